```python
import jax, jax.numpy as jnp
from jax import lax
import numpy as np

D_MODEL = 2048
BATCH = 4
SEQ = 2048
DEPTH = 4
DEC_BATCH = 8
DEC_SEQ = 4
PAST_LEN = 16384
PAGE_SIZE = 128

N_MIXERS = 2
N_FOX = (DEPTH + 1) // 2
N_CONV = DEPTH // 2
N_HEADS = 16
HEAD_DIM = D_MODEL // N_HEADS
D_FF = 2 * D_MODEL
CONV_WIDTH = 3
Q_BLOCK = 128
NORM_EPS = 1e-6
N_NORMS = 6
MACARON_SCALE = 0.5
ATTN_SCALE = HEAD_DIM ** -0.5
FGATE_BIAS_LO = 2.0
FGATE_BIAS_HI = 6.0

kernel_name = 'fox_shortconv_macaron_decode_step'


def rms_norm(x, g):
    xf = x.astype(jnp.float32)
    y = xf * lax.rsqrt(jnp.mean(xf * xf, axis=-1, keepdims=True) + NORM_EPS)
    return (y * g.astype(jnp.float32)).astype(x.dtype)


def swiglu(x, w_up, w_down):
    gate, up = jnp.split(x @ w_up, 2, axis=-1)
    return (jax.nn.silu(gate) * up) @ w_down


def macaron_half(x, g_pre, g_post, w_up, w_down):
    return x + MACARON_SCALE * rms_norm(swiglu(rms_norm(x, g_pre), w_up, w_down), g_post)


def fox_project(a, w_in, b_f):
    bsz, t, _ = a.shape
    q, k, v, f_logit = jnp.split(a @ w_in, [D_MODEL, 2 * D_MODEL, 3 * D_MODEL], axis=-1)
    shp = (bsz, t, N_HEADS, HEAD_DIM)
    logf = jax.nn.log_sigmoid(f_logit.astype(jnp.float32) + b_f.astype(jnp.float32))
    return q.reshape(shp), k.reshape(shp), v.reshape(shp), logf


def fox_attend_prompt(q, k, v, logf):
    bsz, s_len, _, _ = q.shape
    n_blk = s_len // Q_BLOCK
    c = jnp.cumsum(logf, axis=1).transpose(0, 2, 1)
    kf = k.astype(jnp.float32)
    vf = v.astype(jnp.float32)
    q_blocks = (q.astype(jnp.float32) * ATTN_SCALE).reshape(
        bsz, n_blk, Q_BLOCK, N_HEADS, HEAD_DIM).transpose(1, 0, 2, 3, 4)
    c_blocks = c.reshape(bsz, N_HEADS, n_blk, Q_BLOCK).transpose(2, 0, 1, 3)
    key_pos = jnp.arange(s_len)

    def one_block(args):
        blk, qb, cb = args
        q_pos = blk * Q_BLOCK + jnp.arange(Q_BLOCK)
        logits = jnp.einsum('bqhd,bkhd->bhqk', qb, kf) + (cb[..., :, None] - c[:, :, None, :])
        logits = jnp.where(key_pos[None, :] <= q_pos[:, None], logits, -jnp.inf)
        p = jax.nn.softmax(logits, axis=-1)
        return jnp.einsum('bhqk,bkhd->bqhd', p, vf)

    o = lax.map(one_block, (jnp.arange(n_blk), q_blocks, c_blocks))
    return o.transpose(1, 0, 2, 3, 4).reshape(bsz, s_len, D_MODEL).astype(q.dtype)


def fox_attend_sample(q, k, v, logf, k_past, v_past, logf_past):
    bsz, t, _, _ = q.shape
    past = k_past.shape[1]
    lf_all = jnp.concatenate([logf_past.astype(jnp.float32), logf], axis=1)
    c = jnp.cumsum(lf_all, axis=1).transpose(0, 2, 1)
    qf = q.astype(jnp.float32) * ATTN_SCALE
    logits = jnp.concatenate([
        jnp.einsum('bqhd,bkhd->bhqk', qf, k_past.astype(jnp.float32)),
        jnp.einsum('bqhd,bkhd->bhqk', qf, k.astype(jnp.float32))], axis=-1)
    logits = logits + (c[:, :, past:, None] - c[:, :, None, :])
    q_pos = past + jnp.arange(t)
    key_pos = jnp.arange(past + t)
    logits = jnp.where(key_pos[None, :] <= q_pos[:, None], logits, -jnp.inf)
    p = jax.nn.softmax(logits, axis=-1)
    o = (jnp.einsum('bhqk,bkhd->bqhd', p[..., :past], v_past.astype(jnp.float32))
         + jnp.einsum('bhqk,bkhd->bqhd', p[..., past:], v.astype(jnp.float32)))
    return o.reshape(bsz, t, D_MODEL).astype(q.dtype)


def fox_sublayer(x, g_pre, g_post, w_in, b_f, w_out, past):
    a = rms_norm(x, g_pre)
    q, k, v, logf = fox_project(a, w_in, b_f)
    if past is None:
        o = fox_attend_prompt(q, k, v, logf)
    else:
        o = fox_attend_sample(q, k, v, logf, past[0], past[1], past[2])
    return x + rms_norm(o @ w_out, g_post), k, v, logf


def conv_sublayer(x, g_pre, g_post, w_in, conv_w, w_out, u_prev):
    a = rms_norm(x, g_pre)
    gate_b, gate_c, h = jnp.split(a @ w_in, 3, axis=-1)
    u = gate_c * h
    t = u.shape[1]
    u_ext = jnp.concatenate([u_prev.astype(u.dtype), u], axis=1)
    y = conv_w[0] * u_ext[:, 0:t]
    for j in range(1, CONV_WIDTH):
        y = y + conv_w[j] * u_ext[:, j:j + t]
    out = (gate_b * y) @ w_out
    return x + rms_norm(out, g_post), u_ext[:, t:]


def setup_inputs(seed: int = 0) -> dict:
    key = jax.random.key(seed)
    ks = jax.random.split(key, 16)
    n_pages = PAST_LEN // PAGE_SIZE
    n_used = DEC_BATCH * n_pages
    n_pool = (n_used * 5) // 4
    nrm = jax.random.normal
    x_prompt = nrm(ks[0], (BATCH, SEQ, D_MODEL), jnp.float32)
    x_sample = nrm(ks[1], (DEC_BATCH, DEC_SEQ, D_MODEL), jnp.float32)
    cache_k = nrm(ks[2], (N_FOX, n_pool, PAGE_SIZE, N_HEADS, HEAD_DIM), jnp.float32)
    cache_v = nrm(ks[3], (N_FOX, n_pool, PAGE_SIZE, N_HEADS, HEAD_DIM), jnp.float32)
    cache_logf = jax.nn.log_sigmoid(jax.random.uniform(
        ks[4], (N_FOX, n_pool, PAGE_SIZE, N_HEADS), jnp.float32, FGATE_BIAS_LO, FGATE_BIAS_HI))
    state_conv = nrm(ks[5], (N_CONV, DEC_BATCH, CONV_WIDTH - 1, D_MODEL), jnp.float32)
    page_table = jax.random.permutation(ks[6], n_pool)[:n_used].reshape(DEC_BATCH, n_pages).astype(jnp.int32)
    norm_g = 1.0 + 0.05 * nrm(ks[7], (DEPTH, N_NORMS, D_MODEL), jnp.float32)
    w_ffn_up = nrm(ks[8], (DEPTH, 2, D_MODEL, 2 * D_FF), jnp.float32) * D_MODEL ** -0.5
    w_ffn_down = nrm(ks[9], (DEPTH, 2, D_FF, D_MODEL), jnp.float32) * D_FF ** -0.5
    w_fox_in = nrm(ks[10], (N_FOX, D_MODEL, 3 * D_MODEL + N_HEADS), jnp.float32) * D_MODEL ** -0.5
    b_fox_f = jax.random.uniform(ks[11], (N_FOX, N_HEADS), jnp.float32, FGATE_BIAS_LO, FGATE_BIAS_HI)
    w_fox_out = nrm(ks[12], (N_FOX, D_MODEL, D_MODEL), jnp.float32) * D_MODEL ** -0.5
    w_conv_in = nrm(ks[13], (N_CONV, D_MODEL, 3 * D_MODEL), jnp.float32) * D_MODEL ** -0.5
    w_conv_w = nrm(ks[14], (N_CONV, CONV_WIDTH, D_MODEL), jnp.float32) * CONV_WIDTH ** -0.5
    w_conv_out = nrm(ks[15], (N_CONV, D_MODEL, D_MODEL), jnp.float32) * D_MODEL ** -0.5
    return {'x_prompt': x_prompt, 'x_sample': x_sample, 'cache_k': cache_k, 'cache_v': cache_v,
            'cache_logf': cache_logf, 'state_conv': state_conv, 'page_table': page_table,
            'norm_g': norm_g, 'w_ffn_up': w_ffn_up, 'w_ffn_down': w_ffn_down,
            'w_fox_in': w_fox_in, 'b_fox_f': b_fox_f, 'w_fox_out': w_fox_out,
            'w_conv_in': w_conv_in, 'w_conv_w': w_conv_w, 'w_conv_out': w_conv_out}


def reference(x_prompt, x_sample, cache_k, cache_v, cache_logf, state_conv, page_table,
              norm_g, w_ffn_up, w_ffn_down, w_fox_in, b_fox_f, w_fox_out,
              w_conv_in, w_conv_w, w_conv_out):
    n_dec, n_pages = page_table.shape
    past_len = n_pages * PAGE_SIZE
    hp, hs = x_prompt, x_sample
    kp, vp, fp, ks_, vs, fs, cp, cs = [], [], [], [], [], [], [], []
    for i in range(DEPTH):
        g = norm_g[i]
        hp = macaron_half(hp, g[0], g[1], w_ffn_up[i, 0], w_ffn_down[i, 0])
        hs = macaron_half(hs, g[0], g[1], w_ffn_up[i, 0], w_ffn_down[i, 0])
        mi = i // N_MIXERS
        if i % N_MIXERS == 0:
            hp, k_new, v_new, lf_new = fox_sublayer(hp, g[2], g[3], w_fox_in[mi], b_fox_f[mi], w_fox_out[mi], None)
            kp.append(k_new); vp.append(v_new); fp.append(lf_new)
            k_past = cache_k[mi, page_table].reshape(n_dec, past_len, N_HEADS, HEAD_DIM)
            v_past = cache_v[mi, page_table].reshape(n_dec, past_len, N_HEADS, HEAD_DIM)
            lf_past = cache_logf[mi, page_table].reshape(n_dec, past_len, N_HEADS)
            hs, k_new, v_new, lf_new = fox_sublayer(hs, g[2], g[3], w_fox_in[mi], b_fox_f[mi], w_fox_out[mi],
                                                    (k_past, v_past, lf_past))
            ks_.append(k_new); vs.append(v_new); fs.append(lf_new)
        else:
            u0 = jnp.zeros((hp.shape[0], CONV_WIDTH - 1, D_MODEL), hp.dtype)
            hp, u_last = conv_sublayer(hp, g[2], g[3], w_conv_in[mi], w_conv_w[mi], w_conv_out[mi], u0)
            cp.append(u_last)
            hs, u_last = conv_sublayer(hs, g[2], g[3], w_conv_in[mi], w_conv_w[mi], w_conv_out[mi], state_conv[mi])
            cs.append(u_last)
        hp = macaron_half(hp, g[4], g[5], w_ffn_up[i, 1], w_ffn_down[i, 1])
        hs = macaron_half(hs, g[4], g[5], w_ffn_up[i, 1], w_ffn_down[i, 1])
    return (hp, hs, jnp.stack(kp), jnp.stack(vp), jnp.stack(fp), jnp.stack(cp),
            jnp.stack(ks_), jnp.stack(vs), jnp.stack(fs), jnp.stack(cs))
```

```python
import functools

import jax
import jax.numpy as jnp
from jax import lax
from jax.experimental import pallas as pl
from jax.experimental.pallas import tpu as pltpu

F32 = jnp.float32
BF16 = jnp.bfloat16

NORM_EPS = 1e-6
MACARON_SCALE = 0.5
MASK_VALUE = -1e30
V7X_LANES = 128
V7X_SUBLANES = 8
V7X_VMEM_LIMIT_BYTES = 56 * 1024 * 1024


def _params(*semantics):
    return pltpu.CompilerParams(dimension_semantics=semantics,
                                vmem_limit_bytes=V7X_VMEM_LIMIT_BYTES)


def _rms(x, g):
    return x * lax.rsqrt(jnp.mean(x * x, axis=-1, keepdims=True) + NORM_EPS) * g


def _ffn_kernel(x_ref, gpre_ref, gpost_ref, wg_ref, wu_ref, wd_ref, o_ref, a_ref, acc_ref):
    j = pl.program_id(1)

    @pl.when(j == 0)
    def _():
        a_ref[...] = _rms(x_ref[...], gpre_ref[...]).astype(BF16)
        acc_ref[...] = jnp.zeros_like(acc_ref)

    a = a_ref[...]
    gate = jnp.dot(a, wg_ref[...], preferred_element_type=F32)
    up = jnp.dot(a, wu_ref[...], preferred_element_type=F32)
    h = (gate * jax.nn.sigmoid(gate) * up).astype(BF16)
    acc_ref[...] += jnp.dot(h, wd_ref[...], preferred_element_type=F32)

    @pl.when(j == pl.num_programs(1) - 1)
    def _():
        o_ref[...] = x_ref[...] + MACARON_SCALE * _rms(acc_ref[...], gpost_ref[...])


def _ffn_half(x, g_pre, g_post, w_up, w_down, *, tm, tf):
    rows, d = x.shape
    dff = w_down.shape[0]
    nj = dff // tf
    return pl.pallas_call(
        _ffn_kernel,
        grid=(rows // tm, nj),
        in_specs=[
            pl.BlockSpec((tm, d), lambda i, j: (i, 0)),
            pl.BlockSpec((1, d), lambda i, j: (0, 0)),
            pl.BlockSpec((1, d), lambda i, j: (0, 0)),
            pl.BlockSpec((d, tf), lambda i, j: (0, j)),
            pl.BlockSpec((d, tf), lambda i, j: (0, j + nj)),
            pl.BlockSpec((tf, d), lambda i, j: (j, 0)),
        ],
        out_specs=pl.BlockSpec((tm, d), lambda i, j: (i, 0)),
        out_shape=jax.ShapeDtypeStruct((rows, d), F32),
        scratch_shapes=[pltpu.VMEM((tm, d), BF16), pltpu.VMEM((tm, d), F32)],
        compiler_params=_params("arbitrary", "arbitrary"),
        name="ffn_half",
    )(x, g_pre, g_post, w_up, w_up, w_down)


def _nm_kernel(x_ref, g_ref, w_ref, b_ref, o_ref, *, scale, log_sigmoid):
    a = _rms(x_ref[...], g_ref[...]).astype(BF16)
    y = jnp.dot(a, w_ref[...], preferred_element_type=F32)
    if log_sigmoid:
        y = jax.nn.log_sigmoid(y + b_ref[...])
    if scale is not None:
        y = y * scale
    o_ref[...] = y.astype(o_ref.dtype)


def _norm_matmul(x, g, w, *, tm, out_dtype=F32, scale=None, bias=None):
    rows, d = x.shape
    n = w.shape[1]
    log_sigmoid = bias is not None
    if bias is None:
        bias = jnp.zeros((1, n), F32)
    return pl.pallas_call(
        functools.partial(_nm_kernel, scale=scale, log_sigmoid=log_sigmoid),
        grid=(rows // tm,),
        in_specs=[
            pl.BlockSpec((tm, d), lambda i: (i, 0)),
            pl.BlockSpec((1, d), lambda i: (0, 0)),
            pl.BlockSpec((d, n), lambda i: (0, 0)),
            pl.BlockSpec((1, n), lambda i: (0, 0)),
        ],
        out_specs=pl.BlockSpec((tm, n), lambda i: (i, 0)),
        out_shape=jax.ShapeDtypeStruct((rows, n), out_dtype),
        compiler_params=_params("arbitrary"),
        name="norm_matmul",
    )(x, g, w, bias)


def _mnr_kernel(y_ref, w_ref, g_ref, res_ref, o_ref):
    z = jnp.dot(y_ref[...].astype(BF16), w_ref[...], preferred_element_type=F32)
    o_ref[...] = res_ref[...] + _rms(z, g_ref[...])


def _matmul_norm_residual(y, w, g, res, *, tm):
    rows, d = res.shape
    k = w.shape[0]
    return pl.pallas_call(
        _mnr_kernel,
        grid=(rows // tm,),
        in_specs=[
            pl.BlockSpec((tm, k), lambda i: (i, 0)),
            pl.BlockSpec((k, d), lambda i: (0, 0)),
            pl.BlockSpec((1, d), lambda i: (0, 0)),
            pl.BlockSpec((tm, d), lambda i: (i, 0)),
        ],
        out_specs=pl.BlockSpec((tm, d), lambda i: (i, 0)),
        out_shape=jax.ShapeDtypeStruct((rows, d), F32),
        compiler_params=_params("arbitrary"),
        name="matmul_norm_residual",
    )(y, w, g, res)


def _cumsum_kernel(lf_ref, o_ref, carry_ref, *, n_heads):
    tc = lf_ref.shape[0]

    @pl.when(pl.program_id(1) == 0)
    def _():
        carry_ref[...] = jnp.zeros_like(carry_ref)

    lft = lf_ref[...].T
    upper = (lax.broadcasted_iota(jnp.int32, (tc, tc), 0)
             <= lax.broadcasted_iota(jnp.int32, (tc, tc), 1)).astype(F32)
    c = jnp.dot(lft, upper, precision=lax.Precision.HIGHEST,
                preferred_element_type=F32) + carry_ref[...]
    carry_ref[...] = c[:, tc - 1:tc]
    o_ref[0] = c[:n_heads]


def _cumsum_heads(lf, *, batch, seq, n_heads, tc):
    lanes = lf.shape[1]
    nt = seq // tc
    return pl.pallas_call(
        functools.partial(_cumsum_kernel, n_heads=n_heads),
        grid=(batch, nt),
        in_specs=[pl.BlockSpec((tc, lanes), lambda b, t: (b * nt + t, 0))],
        out_specs=pl.BlockSpec((1, n_heads, tc), lambda b, t: (b, 0, t)),
        out_shape=jax.ShapeDtypeStruct((batch, n_heads, seq), F32),
        scratch_shapes=[pltpu.VMEM((lanes, 1), F32)],
        compiler_params=_params("arbitrary", "arbitrary"),
        name="logf_cumsum",
    )(lf)


def _attn_kernel(q_ref, k_ref, v_ref, cs_ref, o_ref, *, tile):
    seq, dh = q_ref.shape
    nq = seq // tile
    h = pl.program_id(1)
    cs = cs_ref[0, pl.ds(h, 1), :]
    row = lax.broadcasted_iota(jnp.int32, (tile, tile), 0)
    col = lax.broadcasted_iota(jnp.int32, (tile, tile), 1)
    causal = col <= row
    for qi in range(nq):
        q = q_ref[qi * tile:(qi + 1) * tile, :]
        m = jnp.full((tile, 1), MASK_VALUE, F32)
        l = jnp.zeros((tile, 1), F32)
        acc = jnp.zeros((tile, dh), F32)
        for j in range(qi + 1):
            kj = k_ref[j * tile:(j + 1) * tile, :].astype(BF16)
            vj = v_ref[j * tile:(j + 1) * tile, :].astype(BF16)
            s = lax.dot_general(q, kj, (((1,), (1,)), ((), ())), preferred_element_type=F32)
            s = s - cs[:, j * tile:(j + 1) * tile]
            if j == qi:
                s = jnp.where(causal, s, MASK_VALUE)
            m_new = jnp.maximum(m, jnp.max(s, axis=-1, keepdims=True))
            alpha = jnp.exp(m - m_new)
            p = jnp.exp(s - m_new)
            l = alpha * l + jnp.sum(p, axis=-1, keepdims=True)
            acc = alpha * acc + jnp.dot(p.astype(BF16), vj, preferred_element_type=F32)
            m = m_new
        o_ref[qi * tile:(qi + 1) * tile, :] = (acc / l).astype(o_ref.dtype)


def _attend_prompt(q, k, v, cs, *, batch, seq, n_heads, tile):
    rows, d = q.shape
    dh = d // n_heads
    blk = lambda b, h: (b, h)
    return pl.pallas_call(
        functools.partial(_attn_kernel, tile=tile),
        grid=(batch, n_heads),
        in_specs=[
            pl.BlockSpec((seq, dh), blk),
            pl.BlockSpec((seq, dh), blk),
            pl.BlockSpec((seq, dh), blk),
            pl.BlockSpec((1, n_heads, seq), lambda b, h: (b, 0, 0)),
        ],
        out_specs=pl.BlockSpec((seq, dh), blk),
        out_shape=jax.ShapeDtypeStruct((rows, d), BF16),
        compiler_params=_params("arbitrary", "arbitrary"),
        name="fox_attend_prompt",
    )(q, k, v, cs)


def _conv_mix(u, um1, um2, b, cw):
    return b * (cw[0:1] * um2 + cw[1:2] * um1 + cw[2:3] * u)


def _conv_prompt_kernel(x_ref, g_ref, wb_ref, wc_ref, wh_ref, cw_ref, st_ref,
                        y_ref, tail_ref, carry_ref, *, tiles_per_seq):
    tm, tn = y_ref.shape
    i = pl.program_id(1)
    a = _rms(x_ref[...], g_ref[...]).astype(BF16)
    b = jnp.dot(a, wb_ref[...], preferred_element_type=F32)
    c = jnp.dot(a, wc_ref[...], preferred_element_type=F32)
    hh = jnp.dot(a, wh_ref[...], preferred_element_type=F32)
    u = c * hh

    @pl.when(i % tiles_per_seq == 0)
    def _():
        carry_ref[...] = st_ref[0]

    prev2 = carry_ref[V7X_SUBLANES - 2:V7X_SUBLANES - 1, :]
    prev1 = carry_ref[V7X_SUBLANES - 1:V7X_SUBLANES, :]
    row = lax.broadcasted_iota(jnp.int32, (tm, tn), 0)
    um1 = jnp.where(row == 0, prev1, pltpu.roll(u, 1, 0))
    um2 = jnp.where(row == 0, prev2, jnp.where(row == 1, prev1, pltpu.roll(u, 2, 0)))
    y_ref[...] = _conv_mix(u, um1, um2, b, cw_ref[...]).astype(y_ref.dtype)
    tail = u[tm - V7X_SUBLANES:tm, :]
    carry_ref[...] = tail
    tail_ref[0] = tail


def _conv_prompt(x, g, w_in, conv_w, state8, *, batch, seq, tm, tn):
    rows, d = x.shape
    nj = d // tn
    tiles_per_seq = seq // tm
    return pl.pallas_call(
        functools.partial(_conv_prompt_kernel, tiles_per_seq=tiles_per_seq),
        grid=(nj, rows // tm),
        in_specs=[
            pl.BlockSpec((tm, d), lambda j, i: (i, 0)),
            pl.BlockSpec((1, d), lambda j, i: (0, 0)),
            pl.BlockSpec((d, tn), lambda j, i: (0, j)),
            pl.BlockSpec((d, tn), lambda j, i: (0, j + nj)),
            pl.BlockSpec((d, tn), lambda j, i: (0, j + 2 * nj)),
            pl.BlockSpec((conv_w.shape[0], tn), lambda j, i: (0, j)),
            pl.BlockSpec((1, V7X_SUBLANES, tn), lambda j, i: (i // tiles_per_seq, 0, j)),
        ],
        out_specs=[
            pl.BlockSpec((tm, tn), lambda j, i: (i, j)),
            pl.BlockSpec((1, V7X_SUBLANES, tn), lambda j, i: (i // tiles_per_seq, 0, j)),
        ],
        out_shape=[jax.ShapeDtypeStruct((rows, d), BF16),
                   jax.ShapeDtypeStruct((batch, V7X_SUBLANES, d), F32)],
        scratch_shapes=[pltpu.VMEM((V7X_SUBLANES, tn), F32)],
        compiler_params=_params("arbitrary", "arbitrary"),
        name="conv_mix_prompt",
    )(x, g, w_in, w_in, w_in, conv_w, state8)


def _conv_sample_kernel(x_ref, g_ref, wb_ref, wc_ref, wh_ref, cw_ref, init1_ref, init2_ref,
                        y_ref, u_ref, *, seq):
    rows, tn = y_ref.shape
    a = _rms(x_ref[...], g_ref[...]).astype(BF16)
    b = jnp.dot(a, wb_ref[...], preferred_element_type=F32)
    c = jnp.dot(a, wc_ref[...], preferred_element_type=F32)
    hh = jnp.dot(a, wh_ref[...], preferred_element_type=F32)
    u = c * hh
    pos = lax.rem(lax.broadcasted_iota(jnp.int32, (rows, tn), 0), seq)
    um1 = jnp.where(pos >= 1, pltpu.roll(u, 1, 0), init1_ref[...])
    um2 = jnp.where(pos >= 2, pltpu.roll(u, 2, 0), init2_ref[...])
    y_ref[...] = _conv_mix(u, um1, um2, b, cw_ref[...]).astype(y_ref.dtype)
    u_ref[...] = u


def _conv_sample(x, g, w_in, conv_w, init1, init2, *, seq, tn):
    rows, d = x.shape
    nj = d // tn
    return pl.pallas_call(
        functools.partial(_conv_sample_kernel, seq=seq),
        grid=(nj,),
        in_specs=[
            pl.BlockSpec((rows, d), lambda j: (0, 0)),
            pl.BlockSpec((1, d), lambda j: (0, 0)),
            pl.BlockSpec((d, tn), lambda j: (0, j)),
            pl.BlockSpec((d, tn), lambda j: (0, j + nj)),
            pl.BlockSpec((d, tn), lambda j: (0, j + 2 * nj)),
            pl.BlockSpec((conv_w.shape[0], tn), lambda j: (0, j)),
            pl.BlockSpec((rows, tn), lambda j: (0, j)),
            pl.BlockSpec((rows, tn), lambda j: (0, j)),
        ],
        out_specs=[pl.BlockSpec((rows, tn), lambda j: (0, j)),
                   pl.BlockSpec((rows, tn), lambda j: (0, j))],
        out_shape=[jax.ShapeDtypeStruct((rows, d), BF16),
                   jax.ShapeDtypeStruct((rows, d), F32)],
        compiler_params=_params("arbitrary"),
        name="conv_mix_sample",
    )(x, g, w_in, w_in, w_in, conv_w, init1, init2)


def _decode_kernel(pt_ref, q_ref, kn_ref, vn_ref, lfn_ref, *rest, n_heads, pages_per_step):
    del pt_ref
    pp = pages_per_step
    k_refs, v_refs, lf_refs = rest[:pp], rest[pp:2 * pp], rest[2 * pp:3 * pp]
    o_ref, qbd_ref, m_ref, l_ref, acc_ref, carry_ref = rest[3 * pp:]
    n_tok, d = q_ref.shape[1], q_ref.shape[2]
    dh = d // n_heads
    n_rows = n_tok * n_heads
    page = k_refs[0].shape[2]
    p_idx = pl.program_id(1)

    row_head = lax.rem(lax.broadcasted_iota(jnp.int32, (n_heads, d), 0), n_heads)
    col_head = lax.broadcasted_iota(jnp.int32, (n_heads, d), 1) // dh
    diag = row_head == col_head

    @pl.when(p_idx == 0)
    def _():
        q = q_ref[0].astype(F32)
        for t in range(n_tok):
            blk = jnp.where(diag, jnp.broadcast_to(q[t:t + 1, :], (n_heads, d)), 0.0)
            qbd_ref[t * n_heads:(t + 1) * n_heads, :] = blk.astype(BF16)
        m_ref[...] = jnp.full_like(m_ref, MASK_VALUE)
        l_ref[...] = jnp.zeros_like(l_ref)
        acc_ref[...] = jnp.zeros_like(acc_ref)
        carry_ref[...] = jnp.zeros_like(carry_ref)

    upper = (lax.broadcasted_iota(jnp.int32, (page, page), 0)
             <= lax.broadcasted_iota(jnp.int32, (page, page), 1)).astype(F32)

    def process(k_page, v_page, lf_t, mask):
        s = lax.dot_general(qbd_ref[...], k_page.astype(BF16), (((1,), (1,)), ((), ())),
                            preferred_element_type=F32)
        cum = jnp.dot(lf_t, upper, precision=lax.Precision.HIGHEST,
                      preferred_element_type=F32) + carry_ref[...]
        carry_ref[...] = cum[:, page - 1:page]
        s = s - cum
        if mask is not None:
            s = jnp.where(mask, s, MASK_VALUE)
        m_old = m_ref[...]
        m_new = jnp.maximum(m_old, jnp.max(s, axis=-1, keepdims=True))
        alpha = jnp.exp(m_old - m_new)
        p = jnp.exp(s - m_new)
        l_ref[...] = alpha * l_ref[...] + jnp.sum(p, axis=-1, keepdims=True)
        acc_ref[...] = alpha * acc_ref[...] + jnp.dot(p.astype(BF16), v_page.astype(BF16),
                                                      preferred_element_type=F32)
        m_ref[...] = m_new

    for r in range(pp):
        process(k_refs[r][0, 0], v_refs[r][0, 0], lf_refs[r][0, 0], None)

    @pl.when(p_idx == pl.num_programs(1) - 1)
    def _():
        key = lax.broadcasted_iota(jnp.int32, (n_rows, page), 1)
        tok = lax.broadcasted_iota(jnp.int32, (n_rows, page), 0) // n_heads
        process(kn_ref[0], vn_ref[0], lfn_ref[0], key <= tok)
        o = acc_ref[...] / l_ref[...]
        for t in range(n_tok):
            blk = jnp.where(diag, o[t * n_heads:(t + 1) * n_heads, :], 0.0)
            o_ref[0, t:t + 1, :] = jnp.sum(blk, axis=0, keepdims=True)


def _attend_sample(q, k_new, v_new, lf_new_t, cache_k, cache_v, cache_lf_t, page_table, *,
                   layer, n_heads, pages_per_step):
    n_b, n_tok, d = q.shape
    page = cache_k.shape[2]
    n_rows = n_tok * n_heads
    n_pages = page_table.shape[1]
    pp = pages_per_step
    n_steps = n_pages // pp

    def page_spec(shape, r):
        return pl.BlockSpec((1, 1) + shape,
                            lambda b, p, pt: (layer, pt[b, p * pp + r], 0, 0))

    in_specs = [
        pl.BlockSpec((1, n_tok, d), lambda b, p, pt: (b, 0, 0)),
        pl.BlockSpec((1, page, d), lambda b, p, pt: (b, 0, 0)),
        pl.BlockSpec((1, page, d), lambda b, p, pt: (b, 0, 0)),
        pl.BlockSpec((1, n_rows, page), lambda b, p, pt: (b, 0, 0)),
    ]
    in_specs += [page_spec((page, d), r) for r in range(pp)]
    in_specs += [page_spec((page, d), r) for r in range(pp)]
    in_specs += [page_spec((n_rows, page), r) for r in range(pp)]
    grid_spec = pltpu.PrefetchScalarGridSpec(
        num_scalar_prefetch=1,
        grid=(n_b, n_steps),
        in_specs=in_specs,
        out_specs=pl.BlockSpec((1, n_tok, d), lambda b, p, pt: (b, 0, 0)),
        scratch_shapes=[
            pltpu.VMEM((n_rows, d), BF16),
            pltpu.VMEM((n_rows, 1), F32),
            pltpu.VMEM((n_rows, 1), F32),
            pltpu.VMEM((n_rows, d), F32),
            pltpu.VMEM((n_rows, 1), F32),
        ],
    )
    return pl.pallas_call(
        functools.partial(_decode_kernel, n_heads=n_heads, pages_per_step=pp),
        grid_spec=grid_spec,
        out_shape=jax.ShapeDtypeStruct((n_b, n_tok, d), F32),
        compiler_params=_params("arbitrary", "arbitrary"),
        name="fox_attend_sample",
    )(page_table, q, k_new, v_new, lf_new_t,
      *([cache_k] * pp), *([cache_v] * pp), *([cache_lf_t] * pp))


def kernel(x_prompt, x_sample, cache_k, cache_v, cache_logf, state_conv, page_table, norm_g,
           w_ffn_up, w_ffn_down, w_fox_in, b_fox_f, w_fox_out, w_conv_in, w_conv_w, w_conv_out):
    n_b, seq, d = x_prompt.shape
    n_db, n_tok, _ = x_sample.shape
    depth = norm_g.shape[0]
    n_heads = b_fox_f.shape[1]
    dh = d // n_heads
    n_pool, page = cache_k.shape[1], cache_k.shape[2]
    attn_scale = dh ** -0.5
    lane_rep = V7X_LANES // n_heads
    conv_width = w_conv_w.shape[1]

    tm_p = 512
    rows_s = n_db * n_tok

    hp = x_prompt.reshape(n_b * seq, d)
    hs = x_sample.reshape(rows_s, d)

    cache_k4 = cache_k.reshape(cache_k.shape[0], n_pool, page, d)
    cache_v4 = cache_v.reshape(cache_v.shape[0], n_pool, page, d)
    cache_lf_t = jnp.tile(jnp.swapaxes(cache_logf, 2, 3), (1, 1, n_tok, 1))

    kp, vp, fp, ksl, vsl, fsl, cp, csl = [], [], [], [], [], [], [], []
    for i in range(depth):
        g = norm_g[i][:, None, :]
        mi = i // 2

        wu, wd = w_ffn_up[i, 0].astype(BF16), w_ffn_down[i, 0].astype(BF16)
        hp = _ffn_half(hp, g[0], g[1], wu, wd, tm=tm_p, tf=512)
        hs = _ffn_half(hs, g[0], g[1], wu, wd, tm=rows_s, tf=1024)

        if i % 2 == 0:
            w_in = w_fox_in[mi]
            wq = w_in[:, :d].astype(BF16)
            wk = w_in[:, d:2 * d].astype(BF16)
            wv = w_in[:, 2 * d:3 * d].astype(BF16)
            wf = jnp.tile(w_in[:, 3 * d:], (1, lane_rep)).astype(BF16)
            bf = jnp.tile(b_fox_f[mi], lane_rep)[None, :]
            wo = w_fox_out[mi].astype(BF16)

            q = _norm_matmul(hp, g[2], wq, tm=tm_p, out_dtype=BF16, scale=attn_scale)
            k = _norm_matmul(hp, g[2], wk, tm=tm_p)
            v = _norm_matmul(hp, g[2], wv, tm=tm_p)
            lf = _norm_matmul(hp, g[2], wf, tm=tm_p, bias=bf)
            cs = _cumsum_heads(lf, batch=n_b, seq=seq, n_heads=n_heads, tc=512)
            o = _attend_prompt(q, k, v, cs, batch=n_b, seq=seq, n_heads=n_heads, tile=512)
            hp = _matmul_norm_residual(o, wo, g[3], hp, tm=tm_p)
            kp.append(k.reshape(n_b, seq, n_heads, dh))
            vp.append(v.reshape(n_b, seq, n_heads, dh))
            fp.append(lf[:, :n_heads].reshape(n_b, seq, n_heads))

            q = _norm_matmul(hs, g[2], wq, tm=rows_s, out_dtype=BF16, scale=attn_scale)
            k = _norm_matmul(hs, g[2], wk, tm=rows_s)
            v = _norm_matmul(hs, g[2], wv, tm=rows_s)
            lf = _norm_matmul(hs, g[2], wf, tm=rows_s, bias=bf)
            pad = ((0, 0), (0, page - n_tok), (0, 0))
            k_new = jnp.pad(k.reshape(n_db, n_tok, d), pad)
            v_new = jnp.pad(v.reshape(n_db, n_tok, d), pad)
            lf_new = lf[:, :n_heads].reshape(n_db, n_tok, n_heads)
            lf_new_t = jnp.tile(
                jnp.pad(jnp.swapaxes(lf_new, 1, 2), ((0, 0), (0, 0), (0, page - n_tok))),
                (1, n_tok, 1))
            o = _attend_sample(q.reshape(n_db, n_tok, d), k_new, v_new, lf_new_t,
                               cache_k4, cache_v4, cache_lf_t, page_table,
                               layer=mi, n_heads=n_heads, pages_per_step=1)
            hs = _matmul_norm_residual(o.reshape(rows_s, d), wo, g[3], hs, tm=rows_s)
            ksl.append(k.reshape(n_db, n_tok, n_heads, dh))
            vsl.append(v.reshape(n_db, n_tok, n_heads, dh))
            fsl.append(lf_new)
        else:
            w_in = w_conv_in[mi].astype(BF16)
            wo = w_conv_out[mi].astype(BF16)
            cw = w_conv_w[mi]

            state8 = jnp.zeros((n_b, V7X_SUBLANES, d), F32)
            y, tail = _conv_prompt(hp, g[2], w_in, cw, state8, batch=n_b, seq=seq,
                                   tm=tm_p, tn=512)
            hp = _matmul_norm_residual(y, wo, g[3], hp, tm=tm_p)
            cp.append(tail[:, V7X_SUBLANES - (conv_width - 1):])

            st = state_conv[mi]
            zero = jnp.zeros((n_db, n_tok, d), F32)
            init1 = zero.at[:, 0].set(st[:, 1]).reshape(rows_s, d)
            init2 = zero.at[:, 0].set(st[:, 0]).at[:, 1].set(st[:, 1]).reshape(rows_s, d)
            y, u = _conv_sample(hs, g[2], w_in, cw, init1, init2, seq=n_tok, tn=512)
            hs = _matmul_norm_residual(y, wo, g[3], hs, tm=rows_s)
            csl.append(u.reshape(n_db, n_tok, d)[:, n_tok - (conv_width - 1):])

        wu, wd = w_ffn_up[i, 1].astype(BF16), w_ffn_down[i, 1].astype(BF16)
        hp = _ffn_half(hp, g[4], g[5], wu, wd, tm=tm_p, tf=512)
        hs = _ffn_half(hs, g[4], g[5], wu, wd, tm=rows_s, tf=1024)

    return (hp.reshape(n_b, seq, d), hs.reshape(n_db, n_tok, d),
            jnp.stack(kp), jnp.stack(vp), jnp.stack(fp), jnp.stack(cp),
            jnp.stack(ksl), jnp.stack(vsl), jnp.stack(fsl), jnp.stack(csl))
```

```python
import functools

import jax
import jax.numpy as jnp
from jax import lax
from jax.experimental import pallas as pl
from jax.experimental.pallas import tpu as pltpu

F32 = jnp.float32
BF16 = jnp.bfloat16

NORM_EPS = 1e-6
MACARON_SCALE = 0.5
MASK_VALUE = -1e30
V7X_LANES = 128
V7X_SUBLANES = 8
V7X_VMEM_LIMIT_BYTES = 56 * 1024 * 1024


def _params(*semantics):
    return pltpu.CompilerParams(dimension_semantics=semantics,
                                vmem_limit_bytes=V7X_VMEM_LIMIT_BYTES)


def _rms(x, g):
    return x * lax.rsqrt(jnp.mean(x * x, axis=-1, keepdims=True) + NORM_EPS) * g


def _ffn_kernel(x_ref, gpre_ref, gpost_ref, wg_ref, wu_ref, wd_ref, o_ref, a_ref, acc_ref):
    j = pl.program_id(1)

    @pl.when(j == 0)
    def _():
        a_ref[...] = _rms(x_ref[...], gpre_ref[...]).astype(BF16)
        acc_ref[...] = jnp.zeros_like(acc_ref)

    a = a_ref[...]
    gate = jnp.dot(a, wg_ref[...], preferred_element_type=F32)
    up = jnp.dot(a, wu_ref[...], preferred_element_type=F32)
    h = (gate * jax.nn.sigmoid(gate) * up).astype(BF16)
    acc_ref[...] += jnp.dot(h, wd_ref[...], preferred_element_type=F32)

    @pl.when(j == pl.num_programs(1) - 1)
    def _():
        o_ref[...] = x_ref[...] + MACARON_SCALE * _rms(acc_ref[...], gpost_ref[...])


def _ffn_half(x, g_pre, g_post, w_up, w_down, *, tm, tf):
    rows, d = x.shape
    dff = w_down.shape[0]
    nj = dff // tf
    return pl.pallas_call(
        _ffn_kernel,
        grid=(rows // tm, nj),
        in_specs=[
            pl.BlockSpec((tm, d), lambda i, j: (i, 0)),
            pl.BlockSpec((1, d), lambda i, j: (0, 0)),
            pl.BlockSpec((1, d), lambda i, j: (0, 0)),
            pl.BlockSpec((d, tf), lambda i, j: (0, j)),
            pl.BlockSpec((d, tf), lambda i, j: (0, j + nj)),
            pl.BlockSpec((tf, d), lambda i, j: (j, 0)),
        ],
        out_specs=pl.BlockSpec((tm, d), lambda i, j: (i, 0)),
        out_shape=jax.ShapeDtypeStruct((rows, d), F32),
        scratch_shapes=[pltpu.VMEM((tm, d), BF16), pltpu.VMEM((tm, d), F32)],
        compiler_params=_params("arbitrary", "arbitrary"),
        name="ffn_half",
    )(x, g_pre, g_post, w_up, w_up, w_down)


def _nm_kernel(x_ref, g_ref, w_ref, b_ref, *rest, scale, log_sigmoid):
    o_ref = rest[-1]
    a = _rms(x_ref[...], g_ref[...]).astype(BF16)
    y = jnp.dot(a, w_ref[...], preferred_element_type=F32)
    if log_sigmoid:
        y = jax.nn.log_sigmoid(y + b_ref[...])
    if scale is not None:
        y = y * scale
    o_ref[...] = y.astype(o_ref.dtype)


def _norm_matmul(x, g, w, *, tm, n, col_block=0, out_dtype=F32, scale=None, bias=None,
                 stack=None):
    rows, d = x.shape
    log_sigmoid = bias is not None
    if bias is None:
        bias = jnp.zeros((1, n), F32)
    in_specs = [
        pl.BlockSpec((tm, d), lambda i: (i, 0)),
        pl.BlockSpec((1, d), lambda i: (0, 0)),
        pl.BlockSpec((d, n), lambda i: (0, col_block)),
        pl.BlockSpec((1, n), lambda i: (0, 0)),
    ]
    args = [x, g, w, bias]
    aliases = {}
    if stack is None:
        out_spec = pl.BlockSpec((tm, n), lambda i: (i, 0))
        out_shape = jax.ShapeDtypeStruct((rows, n), out_dtype)
    else:
        layer, stacked = stack
        out_spec = pl.BlockSpec((None, tm, n), lambda i: (layer, i, 0))
        out_shape = jax.ShapeDtypeStruct(stacked.shape, stacked.dtype)
        in_specs.append(pl.BlockSpec(memory_space=pl.ANY))
        args.append(stacked)
        aliases = {len(args) - 1: 0}
    return pl.pallas_call(
        functools.partial(_nm_kernel, scale=scale, log_sigmoid=log_sigmoid),
        grid=(rows // tm,),
        in_specs=in_specs,
        out_specs=out_spec,
        out_shape=out_shape,
        input_output_aliases=aliases,
        compiler_params=_params("arbitrary"),
        name="norm_matmul",
    )(*args)


def _kv_kernel(x_ref, g_ref, w_ref, stacked_ref, o_heads_ref, o_flat_ref):
    del stacked_ref
    n_heads, dh = o_heads_ref.shape[1:]
    a = _rms(x_ref[...], g_ref[...]).astype(BF16)
    y = jnp.dot(a, w_ref[...], preferred_element_type=F32)
    o_flat_ref[...] = y.astype(o_flat_ref.dtype)
    for h in range(n_heads):
        o_heads_ref[:, h, :] = y[:, h * dh:(h + 1) * dh]


def _kv_project(x, g, w, stacked, *, tm, col_block, layer):
    rows, d = x.shape
    n_heads, dh = stacked.shape[2:]
    return pl.pallas_call(
        _kv_kernel,
        grid=(rows // tm,),
        in_specs=[
            pl.BlockSpec((tm, d), lambda i: (i, 0)),
            pl.BlockSpec((1, d), lambda i: (0, 0)),
            pl.BlockSpec((d, d), lambda i: (0, col_block)),
            pl.BlockSpec(memory_space=pl.ANY),
        ],
        out_specs=[pl.BlockSpec((None, tm, n_heads, dh), lambda i: (layer, i, 0, 0)),
                   pl.BlockSpec((tm, d), lambda i: (i, 0))],
        out_shape=[jax.ShapeDtypeStruct(stacked.shape, stacked.dtype),
                   jax.ShapeDtypeStruct((rows, d), BF16)],
        input_output_aliases={3: 0},
        compiler_params=_params("arbitrary"),
        name="kv_project",
    )(x, g, w, stacked)


def _mnr_kernel(y_ref, w_ref, g_ref, res_ref, o_ref):
    z = jnp.dot(y_ref[...].astype(BF16), w_ref[...], preferred_element_type=F32)
    o_ref[...] = res_ref[...] + _rms(z, g_ref[...])


def _matmul_norm_residual(y, w, g, res, *, tm):
    rows, d = res.shape
    k = w.shape[0]
    return pl.pallas_call(
        _mnr_kernel,
        grid=(rows // tm,),
        in_specs=[
            pl.BlockSpec((tm, k), lambda i: (i, 0)),
            pl.BlockSpec((k, d), lambda i: (0, 0)),
            pl.BlockSpec((1, d), lambda i: (0, 0)),
            pl.BlockSpec((tm, d), lambda i: (i, 0)),
        ],
        out_specs=pl.BlockSpec((tm, d), lambda i: (i, 0)),
        out_shape=jax.ShapeDtypeStruct((rows, d), F32),
        compiler_params=_params("arbitrary"),
        name="matmul_norm_residual",
    )(y, w, g, res)


def _cumsum_kernel(lf_ref, o_ref, carry_ref, *, n_heads):
    tc = lf_ref.shape[0]

    @pl.when(pl.program_id(1) == 0)
    def _():
        carry_ref[...] = jnp.zeros_like(carry_ref)

    lft = lf_ref[...].T
    upper = (lax.broadcasted_iota(jnp.int32, (tc, tc), 0)
             <= lax.broadcasted_iota(jnp.int32, (tc, tc), 1)).astype(F32)
    c = jnp.dot(lft, upper, precision=lax.Precision.HIGHEST,
                preferred_element_type=F32) + carry_ref[...]
    carry_ref[...] = c[:, tc - 1:tc]
    o_ref[0] = c[:n_heads]


def _cumsum_heads(lf, *, layer, batch, seq, n_heads, tc):
    lanes = lf.shape[2]
    nt = seq // tc
    return pl.pallas_call(
        functools.partial(_cumsum_kernel, n_heads=n_heads),
        grid=(batch, nt),
        in_specs=[pl.BlockSpec((None, tc, lanes), lambda b, t: (layer, b * nt + t, 0))],
        out_specs=pl.BlockSpec((1, n_heads, tc), lambda b, t: (b, 0, t)),
        out_shape=jax.ShapeDtypeStruct((batch, n_heads, seq), F32),
        scratch_shapes=[pltpu.VMEM((lanes, 1), F32)],
        compiler_params=_params("arbitrary", "arbitrary"),
        name="logf_cumsum",
    )(lf)


def _attn_kernel(q_ref, k_ref, v_ref, cs_ref, o_ref, *, tile):
    seq, dh = q_ref.shape
    nq = seq // tile
    h = pl.program_id(1)
    cs = cs_ref[0, pl.ds(h, 1), :]
    row = lax.broadcasted_iota(jnp.int32, (tile, tile), 0)
    col = lax.broadcasted_iota(jnp.int32, (tile, tile), 1)
    causal = col <= row
    for qi in range(nq):
        q = q_ref[qi * tile:(qi + 1) * tile, :]
        m = jnp.full((tile, 1), MASK_VALUE, F32)
        l = jnp.zeros((tile, 1), F32)
        acc = jnp.zeros((tile, dh), F32)
        for j in range(qi + 1):
            kj = k_ref[j * tile:(j + 1) * tile, :]
            vj = v_ref[j * tile:(j + 1) * tile, :]
            s = lax.dot_general(q, kj, (((1,), (1,)), ((), ())), preferred_element_type=F32)
            s = s - cs[:, j * tile:(j + 1) * tile]
            if j == qi:
                s = jnp.where(causal, s, MASK_VALUE)
            m_new = jnp.maximum(m, jnp.max(s, axis=-1, keepdims=True))
            alpha = jnp.exp(m - m_new)
            p = jnp.exp(s - m_new)
            l = alpha * l + jnp.sum(p, axis=-1, keepdims=True)
            acc = alpha * acc + jnp.dot(p.astype(BF16), vj, preferred_element_type=F32)
            m = m_new
        o_ref[qi * tile:(qi + 1) * tile, :] = (acc / l).astype(o_ref.dtype)


def _attend_prompt(q, k, v, cs, *, batch, seq, n_heads, tile):
    rows, d = q.shape
    dh = d // n_heads
    return pl.pallas_call(
        functools.partial(_attn_kernel, tile=tile),
        grid=(batch, n_heads),
        in_specs=[
            pl.BlockSpec((seq, dh), lambda b, h: (b, h)),
            pl.BlockSpec((seq, dh), lambda b, h: (b, h)),
            pl.BlockSpec((seq, dh), lambda b, h: (b, h)),
            pl.BlockSpec((1, n_heads, seq), lambda b, h: (b, 0, 0)),
        ],
        out_specs=pl.BlockSpec((seq, dh), lambda b, h: (b, h)),
        out_shape=jax.ShapeDtypeStruct((rows, d), BF16),
        compiler_params=_params("arbitrary", "arbitrary"),
        name="fox_attend_prompt",
    )(q, k, v, cs)


def _conv_mix(u, um1, um2, b, cw):
    return b * (cw[0:1] * um2 + cw[1:2] * um1 + cw[2:3] * u)


def _conv_prompt_kernel(x_ref, g_ref, wb_ref, wc_ref, wh_ref, cw_ref, st_ref,
                        y_ref, tail_ref, carry_ref, *, tiles_per_seq):
    tm, tn = y_ref.shape
    i = pl.program_id(1)
    a = _rms(x_ref[...], g_ref[...]).astype(BF16)
    b = jnp.dot(a, wb_ref[...], preferred_element_type=F32)
    c = jnp.dot(a, wc_ref[...], preferred_element_type=F32)
    hh = jnp.dot(a, wh_ref[...], preferred_element_type=F32)
    u = c * hh

    @pl.when(i % tiles_per_seq == 0)
    def _():
        carry_ref[...] = st_ref[0]

    prev2 = carry_ref[V7X_SUBLANES - 2:V7X_SUBLANES - 1, :]
    prev1 = carry_ref[V7X_SUBLANES - 1:V7X_SUBLANES, :]
    row = lax.broadcasted_iota(jnp.int32, (tm, tn), 0)
    um1 = jnp.where(row == 0, prev1, pltpu.roll(u, 1, 0))
    um2 = jnp.where(row == 0, prev2, jnp.where(row == 1, prev1, pltpu.roll(u, 2, 0)))
    y_ref[...] = _conv_mix(u, um1, um2, b, cw_ref[...]).astype(y_ref.dtype)
    tail = u[tm - V7X_SUBLANES:tm, :]
    carry_ref[...] = tail
    tail_ref[0] = tail


def _conv_prompt(x, g, w_in, conv_w, state8, *, batch, seq, tm, tn):
    rows, d = x.shape
    nj = d // tn
    tiles_per_seq = seq // tm
    return pl.pallas_call(
        functools.partial(_conv_prompt_kernel, tiles_per_seq=tiles_per_seq),
        grid=(nj, rows // tm),
        in_specs=[
            pl.BlockSpec((tm, d), lambda j, i: (i, 0)),
            pl.BlockSpec((1, d), lambda j, i: (0, 0)),
            pl.BlockSpec((d, tn), lambda j, i: (0, j)),
            pl.BlockSpec((d, tn), lambda j, i: (0, j + nj)),
            pl.BlockSpec((d, tn), lambda j, i: (0, j + 2 * nj)),
            pl.BlockSpec((conv_w.shape[0], tn), lambda j, i: (0, j)),
            pl.BlockSpec((1, V7X_SUBLANES, tn), lambda j, i: (i // tiles_per_seq, 0, j)),
        ],
        out_specs=[
            pl.BlockSpec((tm, tn), lambda j, i: (i, j)),
            pl.BlockSpec((1, V7X_SUBLANES, tn), lambda j, i: (i // tiles_per_seq, 0, j)),
        ],
        out_shape=[jax.ShapeDtypeStruct((rows, d), BF16),
                   jax.ShapeDtypeStruct((batch, V7X_SUBLANES, d), F32)],
        scratch_shapes=[pltpu.VMEM((V7X_SUBLANES, tn), F32)],
        compiler_params=_params("arbitrary", "arbitrary"),
        name="conv_mix_prompt",
    )(x, g, w_in, w_in, w_in, conv_w, state8)


def _conv_sample_kernel(x_ref, g_ref, wb_ref, wc_ref, wh_ref, cw_ref, init1_ref, init2_ref,
                        y_ref, u_ref, *, seq):
    rows, tn = y_ref.shape
    a = _rms(x_ref[...], g_ref[...]).astype(BF16)
    b = jnp.dot(a, wb_ref[...], preferred_element_type=F32)
    c = jnp.dot(a, wc_ref[...], preferred_element_type=F32)
    hh = jnp.dot(a, wh_ref[...], preferred_element_type=F32)
    u = c * hh
    pos = lax.rem(lax.broadcasted_iota(jnp.int32, (rows, tn), 0), seq)
    um1 = jnp.where(pos >= 1, pltpu.roll(u, 1, 0), init1_ref[...])
    um2 = jnp.where(pos >= 2, pltpu.roll(u, 2, 0), init2_ref[...])
    y_ref[...] = _conv_mix(u, um1, um2, b, cw_ref[...]).astype(y_ref.dtype)
    u_ref[...] = u


def _conv_sample(x, g, w_in, conv_w, init1, init2, *, seq, tn):
    rows, d = x.shape
    nj = d // tn
    return pl.pallas_call(
        functools.partial(_conv_sample_kernel, seq=seq),
        grid=(nj,),
        in_specs=[
            pl.BlockSpec((rows, d), lambda j: (0, 0)),
            pl.BlockSpec((1, d), lambda j: (0, 0)),
            pl.BlockSpec((d, tn), lambda j: (0, j)),
            pl.BlockSpec((d, tn), lambda j: (0, j + nj)),
            pl.BlockSpec((d, tn), lambda j: (0, j + 2 * nj)),
            pl.BlockSpec((conv_w.shape[0], tn), lambda j: (0, j)),
            pl.BlockSpec((rows, tn), lambda j: (0, j)),
            pl.BlockSpec((rows, tn), lambda j: (0, j)),
        ],
        out_specs=[pl.BlockSpec((rows, tn), lambda j: (0, j)),
                   pl.BlockSpec((rows, tn), lambda j: (0, j))],
        out_shape=[jax.ShapeDtypeStruct((rows, d), BF16),
                   jax.ShapeDtypeStruct((rows, d), F32)],
        compiler_params=_params("arbitrary"),
        name="conv_mix_sample",
    )(x, g, w_in, w_in, w_in, conv_w, init1, init2)


def _packed_cumsum(lf, carry, n_heads):
    rows, lanes = lf.shape
    lane = lax.broadcasted_iota(jnp.int32, lf.shape, 1)
    x = lf
    shift = n_heads
    while shift < lanes:
        x = x + jnp.where(lane >= shift, pltpu.roll(x, shift, 1), 0.0)
        shift *= 2
    tot = jnp.where(lane >= lanes - n_heads, x, 0.0)
    shift = n_heads
    while shift < lanes:
        tot = tot + pltpu.roll(tot, shift, 1)
        shift *= 2
    incl = tot
    if rows > 1:
        row = lax.broadcasted_iota(jnp.int32, lf.shape, 0)
        shift = 1
        while shift < rows:
            incl = incl + jnp.where(row >= shift, pltpu.roll(incl, shift, 0), 0.0)
            shift *= 2
    cum = x + (incl - tot) + carry
    return cum, incl[rows - 1:rows, :] + carry


def _decode_kernel(pt_ref, q_ref, kn_ref, vn_ref, lfn_ref, *rest, n_heads, pages_per_step):
    del pt_ref
    pp = pages_per_step
    k_refs, v_refs, lf_refs = rest[:pp], rest[pp:2 * pp], rest[2 * pp:3 * pp]
    o_ref, m_ref, l_ref, acc_ref, carry_ref = rest[3 * pp:]
    n_rows, dh = q_ref.shape[1], q_ref.shape[2]
    lanes = lf_refs[0].shape[3]
    p_idx = pl.program_id(1)

    @pl.when(p_idx == 0)
    def _():
        m_ref[...] = jnp.full_like(m_ref, MASK_VALUE)
        l_ref[...] = jnp.zeros_like(l_ref)
        acc_ref[...] = jnp.zeros_like(acc_ref)
        carry_ref[...] = jnp.zeros_like(carry_ref)

    row = lax.broadcasted_iota(jnp.int32, (n_rows, lanes), 0)
    lane = lax.broadcasted_iota(jnp.int32, (n_rows, lanes), 1)
    same_head = lax.rem(row, n_heads) == lax.rem(lane, n_heads)

    def process(k_blocks, v_blocks, lf, valid):
        q = q_ref[0]
        cum, carry = _packed_cumsum(lf, carry_ref[...], n_heads)
        carry_ref[...] = carry
        parts = []
        c_row = 0
        for kb in k_blocks:
            s = lax.dot_general(q, kb, (((1,), (1,)), ((), ())), preferred_element_type=F32)
            for j in range(kb.shape[0] // lanes):
                sj = s[:, j * lanes:(j + 1) * lanes] - cum[c_row:c_row + 1, :]
                parts.append(jnp.where(valid, sj, MASK_VALUE))
                c_row += 1
        s = jnp.concatenate(parts, axis=1)
        m_old = m_ref[...]
        m_new = jnp.maximum(m_old, jnp.max(s, axis=-1, keepdims=True))
        alpha = jnp.exp(m_old - m_new)
        p = jnp.exp(s - m_new)
        l_ref[...] = alpha * l_ref[...] + jnp.sum(p, axis=-1, keepdims=True)
        pv = None
        col = 0
        for vb in v_blocks:
            part = jnp.dot(p[:, col:col + vb.shape[0]], vb, preferred_element_type=F32)
            pv = part if pv is None else pv + part
            col += vb.shape[0]
        acc_ref[...] = alpha * acc_ref[...] + pv
        m_ref[...] = m_new

    process([r[0, 0] for r in k_refs], [r[0, 0] for r in v_refs],
            jnp.concatenate([r[0, 0] for r in lf_refs], axis=0), same_head)

    @pl.when(p_idx == pl.num_programs(1) - 1)
    def _():
        causal = (lane // n_heads) <= (row // n_heads)
        process([kn_ref[0]], [vn_ref[0]], lfn_ref[0], same_head & causal)
        o_ref[0] = acc_ref[...] / l_ref[...]


def _attend_sample(q, k_new, v_new, lf_new, cache_k, cache_v, cache_lf, page_table, *,
                   layer, n_heads, pages_per_step):
    n_b, n_rows, dh = q.shape
    page_rows = cache_k.shape[2]
    lf_rows, lanes = cache_lf.shape[2:]
    n_pages = page_table.shape[1]
    pp = pages_per_step
    n_steps = n_pages // pp

    def page_spec(rows, cols, r):
        return pl.BlockSpec((1, 1, rows, cols),
                            lambda b, p, pt: (layer, pt[b, p * pp + r], 0, 0))

    in_specs = [
        pl.BlockSpec((1, n_rows, dh), lambda b, p, pt: (b, 0, 0)),
        pl.BlockSpec((1, lanes, dh), lambda b, p, pt: (b, 0, 0)),
        pl.BlockSpec((1, lanes, dh), lambda b, p, pt: (b, 0, 0)),
        pl.BlockSpec((1, 1, lanes), lambda b, p, pt: (b, 0, 0)),
    ]
    in_specs += [page_spec(page_rows, dh, r) for r in range(pp)]
    in_specs += [page_spec(page_rows, dh, r) for r in range(pp)]
    in_specs += [page_spec(lf_rows, lanes, r) for r in range(pp)]
    grid_spec = pltpu.PrefetchScalarGridSpec(
        num_scalar_prefetch=1,
        grid=(n_b, n_steps),
        in_specs=in_specs,
        out_specs=pl.BlockSpec((1, n_rows, dh), lambda b, p, pt: (b, 0, 0)),
        scratch_shapes=[
            pltpu.VMEM((n_rows, 1), F32),
            pltpu.VMEM((n_rows, 1), F32),
            pltpu.VMEM((n_rows, dh), F32),
            pltpu.VMEM((1, lanes), F32),
        ],
    )
    return pl.pallas_call(
        functools.partial(_decode_kernel, n_heads=n_heads, pages_per_step=pp),
        grid_spec=grid_spec,
        out_shape=jax.ShapeDtypeStruct((n_b, n_rows, dh), F32),
        compiler_params=_params("arbitrary", "arbitrary"),
        name="fox_attend_sample",
    )(page_table, q, k_new, v_new, lf_new,
      *([cache_k] * pp), *([cache_v] * pp), *([cache_lf] * pp))


def kernel(x_prompt, x_sample, cache_k, cache_v, cache_logf, state_conv, page_table, norm_g,
           w_ffn_up, w_ffn_down, w_fox_in, b_fox_f, w_fox_out, w_conv_in, w_conv_w, w_conv_out):
    n_b, seq, d = x_prompt.shape
    n_db, n_tok, _ = x_sample.shape
    depth = norm_g.shape[0]
    n_fox = w_fox_in.shape[0]
    n_heads = b_fox_f.shape[1]
    dh = d // n_heads
    page = cache_k.shape[2]
    attn_scale = dh ** -0.5
    conv_width = w_conv_w.shape[1]

    tm_p = 512
    rows_p = n_b * seq
    rows_s = n_db * n_tok

    hp = x_prompt.reshape(rows_p, d)
    hs = x_sample.reshape(rows_s, d)

    keys_per_row = V7X_LANES // n_heads
    n_pool = cache_k.shape[1]
    cache_k2 = cache_k.reshape(n_fox, n_pool, page * n_heads, dh)
    cache_v2 = cache_v.reshape(n_fox, n_pool, page * n_heads, dh)
    cache_lf = cache_logf.reshape(n_fox, n_pool, page // keys_per_row, V7X_LANES)
    w_fox_in_bf = w_fox_in.astype(BF16)
    w_fgate = jnp.pad(w_fox_in[:, :, 3 * d:], ((0, 0), (0, 0), (0, V7X_LANES - n_heads)))
    w_fgate = w_fgate.astype(BF16)
    b_fgate = jnp.pad(b_fox_f, ((0, 0), (0, V7X_LANES - n_heads)))[:, None, :]

    k_all = jnp.zeros((n_fox, rows_p, n_heads, dh), F32)
    v_all = jnp.zeros((n_fox, rows_p, n_heads, dh), F32)
    lf_all = jnp.zeros((n_fox, rows_p, V7X_LANES), F32)
    ksl, vsl, fsl, cp, csl = [], [], [], [], []
    for i in range(depth):
        g = norm_g[i][:, None, :]
        mi = i // 2

        wu, wd = w_ffn_up[i, 0].astype(BF16), w_ffn_down[i, 0].astype(BF16)
        hp = _ffn_half(hp, g[0], g[1], wu, wd, tm=tm_p, tf=512)
        hs = _ffn_half(hs, g[0], g[1], wu, wd, tm=rows_s, tf=1024)

        if i % 2 == 0:
            w_in = w_fox_in_bf[mi]
            wf, bf = w_fgate[mi], b_fgate[mi]
            wo = w_fox_out[mi].astype(BF16)

            q = _norm_matmul(hp, g[2], w_in, tm=tm_p, n=d, col_block=0, out_dtype=BF16,
                             scale=attn_scale)
            k_all, k_bf = _kv_project(hp, g[2], w_in, k_all, tm=tm_p, col_block=1, layer=mi)
            v_all, v_bf = _kv_project(hp, g[2], w_in, v_all, tm=tm_p, col_block=2, layer=mi)
            lf_all = _norm_matmul(hp, g[2], wf, tm=tm_p, n=V7X_LANES, bias=bf,
                                  stack=(mi, lf_all))
            cs = _cumsum_heads(lf_all, layer=mi, batch=n_b, seq=seq, n_heads=n_heads, tc=512)
            o = _attend_prompt(q, k_bf, v_bf, cs, batch=n_b, seq=seq, n_heads=n_heads,
                               tile=512)
            hp = _matmul_norm_residual(o, wo, g[3], hp, tm=tm_p)

            q = _norm_matmul(hs, g[2], w_in, tm=rows_s, n=d, col_block=0, scale=attn_scale)
            k = _norm_matmul(hs, g[2], w_in, tm=rows_s, n=d, col_block=1)
            v = _norm_matmul(hs, g[2], w_in, tm=rows_s, n=d, col_block=2)
            lf = _norm_matmul(hs, g[2], wf, tm=rows_s, n=V7X_LANES, bias=bf)
            pad = ((0, 0), (0, keys_per_row - n_tok), (0, 0), (0, 0))
            k_new = jnp.pad(k.reshape(n_db, n_tok, n_heads, dh), pad)
            v_new = jnp.pad(v.reshape(n_db, n_tok, n_heads, dh), pad)
            lf_new = lf[:, :n_heads].reshape(n_db, n_tok, n_heads)
            o = _attend_sample(q.reshape(n_db, n_tok * n_heads, dh),
                               k_new.reshape(n_db, V7X_LANES, dh),
                               v_new.reshape(n_db, V7X_LANES, dh),
                               jnp.pad(lf_new, pad[:3]).reshape(n_db, 1, V7X_LANES),
                               cache_k2, cache_v2, cache_lf, page_table,
                               layer=mi, n_heads=n_heads, pages_per_step=4)
            hs = _matmul_norm_residual(o.reshape(rows_s, d), wo, g[3], hs, tm=rows_s)
            ksl.append(k.reshape(n_db, n_tok, n_heads, dh))
            vsl.append(v.reshape(n_db, n_tok, n_heads, dh))
            fsl.append(lf_new)
        else:
            w_in = w_conv_in[mi].astype(BF16)
            wo = w_conv_out[mi].astype(BF16)
            cw = w_conv_w[mi]

            state8 = jnp.zeros((n_b, V7X_SUBLANES, d), F32)
            y, tail = _conv_prompt(hp, g[2], w_in, cw, state8, batch=n_b, seq=seq,
                                   tm=tm_p, tn=512)
            hp = _matmul_norm_residual(y, wo, g[3], hp, tm=tm_p)
            cp.append(tail[:, V7X_SUBLANES - (conv_width - 1):])

            st = state_conv[mi]
            zero = jnp.zeros((n_db, n_tok, d), F32)
            init1 = zero.at[:, 0].set(st[:, 1]).reshape(rows_s, d)
            init2 = zero.at[:, 0].set(st[:, 0]).at[:, 1].set(st[:, 1]).reshape(rows_s, d)
            y, u = _conv_sample(hs, g[2], w_in, cw, init1, init2, seq=n_tok, tn=512)
            hs = _matmul_norm_residual(y, wo, g[3], hs, tm=rows_s)
            csl.append(u.reshape(n_db, n_tok, d)[:, n_tok - (conv_width - 1):])

        wu, wd = w_ffn_up[i, 1].astype(BF16), w_ffn_down[i, 1].astype(BF16)
        hp = _ffn_half(hp, g[4], g[5], wu, wd, tm=tm_p, tf=512)
        hs = _ffn_half(hs, g[4], g[5], wu, wd, tm=rows_s, tf=1024)

    return (hp.reshape(n_b, seq, d), hs.reshape(n_db, n_tok, d),
            k_all.reshape(n_fox, n_b, seq, n_heads, dh),
            v_all.reshape(n_fox, n_b, seq, n_heads, dh),
            lf_all[:, :, :n_heads].reshape(n_fox, n_b, seq, n_heads),
            jnp.stack(cp), jnp.stack(ksl), jnp.stack(vsl), jnp.stack(fsl), jnp.stack(csl))
```

```python
import functools
import math

import jax
import jax.numpy as jnp
from jax import lax
from jax.experimental import pallas as pl
from jax.experimental.pallas import tpu as pltpu

F32 = jnp.float32
BF16 = jnp.bfloat16

NORM_EPS = 1e-6
MACARON_SCALE = 0.5
MASK_VALUE = -1e30
LOG2_E = math.log2(math.e)
V7X_LANES = 128
V7X_SUBLANES = 8
V7X_VMEM_LIMIT_BYTES = 56 * 1024 * 1024
NT_DIMS = (((1,), (1,)), ((), ()))


def _params(*semantics):
    return pltpu.CompilerParams(dimension_semantics=semantics,
                                vmem_limit_bytes=V7X_VMEM_LIMIT_BYTES)


def _rms(x, g):
    return x * lax.rsqrt(jnp.mean(x * x, axis=-1, keepdims=True) + NORM_EPS) * g


def _ffn_kernel(x_ref, gpre_ref, gpost_ref, wg_ref, wu_ref, wd_ref, o_ref, a_ref, acc_ref):
    j = pl.program_id(1)
    last = pl.num_programs(1) - 1

    def partial_out(a):
        gate = jnp.dot(a, wg_ref[...], preferred_element_type=F32)
        up = jnp.dot(a, wu_ref[...], preferred_element_type=F32)
        h = (gate * jax.nn.sigmoid(gate) * up).astype(BF16)
        return jnp.dot(h, wd_ref[...], preferred_element_type=F32)

    @pl.when(j == 0)
    def _():
        a = _rms(x_ref[...], gpre_ref[...]).astype(BF16)
        a_ref[...] = a
        acc_ref[...] = partial_out(a)

    @pl.when(jnp.logical_and(j > 0, j < last))
    def _():
        acc_ref[...] += partial_out(a_ref[...])

    @pl.when(j == last)
    def _():
        y = acc_ref[...] + partial_out(a_ref[...])
        o_ref[...] = x_ref[...] + MACARON_SCALE * _rms(y, gpost_ref[...])


def _ffn_half(x, g_pre, g_post, w_up, w_down, *, layer, half, tm, tf):
    rows, d = x.shape
    dff = w_down.shape[2]
    nj = dff // tf
    assert nj >= 2
    return pl.pallas_call(
        _ffn_kernel,
        grid=(rows // tm, nj),
        in_specs=[
            pl.BlockSpec((tm, d), lambda i, j: (i, 0)),
            pl.BlockSpec((1, d), lambda i, j: (0, 0)),
            pl.BlockSpec((1, d), lambda i, j: (0, 0)),
            pl.BlockSpec((None, None, d, tf), lambda i, j: (layer, half, 0, j)),
            pl.BlockSpec((None, None, d, tf), lambda i, j: (layer, half, 0, j + nj)),
            pl.BlockSpec((None, None, tf, d), lambda i, j: (layer, half, j, 0)),
        ],
        out_specs=pl.BlockSpec((tm, d), lambda i, j: (i, 0)),
        out_shape=jax.ShapeDtypeStruct((rows, d), F32),
        scratch_shapes=[pltpu.VMEM((tm, d), BF16), pltpu.VMEM((tm, d), F32)],
        compiler_params=_params("arbitrary", "arbitrary"),
        name="ffn_half",
    )(x, g_pre, g_post, w_up, w_up, w_down)


def _nm_kernel(x_ref, g_ref, w_ref, b_ref, *rest, scale, log_sigmoid):
    o_ref = rest[-1]
    a = _rms(x_ref[...], g_ref[...]).astype(BF16)
    y = jnp.dot(a, w_ref[...], preferred_element_type=F32)
    if log_sigmoid:
        y = jax.nn.log_sigmoid(y + b_ref[...])
    if scale is not None:
        y = y * scale
    o_ref[...] = y.astype(o_ref.dtype)


def _norm_matmul(x, g, w, *, layer, tm, n, col_block=0, out_dtype=F32, scale=None, bias=None,
                 stack=None):
    rows, d = x.shape
    log_sigmoid = bias is not None
    if bias is None:
        bias = jnp.zeros((1, n), F32)
    in_specs = [
        pl.BlockSpec((tm, d), lambda i: (i, 0)),
        pl.BlockSpec((1, d), lambda i: (0, 0)),
        pl.BlockSpec((None, d, n), lambda i: (layer, 0, col_block)),
        pl.BlockSpec((1, n), lambda i: (0, 0)),
    ]
    args = [x, g, w, bias]
    aliases = {}
    if stack is None:
        out_spec = pl.BlockSpec((tm, n), lambda i: (i, 0))
        out_shape = jax.ShapeDtypeStruct((rows, n), out_dtype)
    else:
        slab, stacked = stack
        out_spec = pl.BlockSpec((None, tm, n), lambda i: (slab, i, 0))
        out_shape = jax.ShapeDtypeStruct(stacked.shape, stacked.dtype)
        in_specs.append(pl.BlockSpec(memory_space=pl.ANY))
        args.append(stacked)
        aliases = {len(args) - 1: 0}
    return pl.pallas_call(
        functools.partial(_nm_kernel, scale=scale, log_sigmoid=log_sigmoid),
        grid=(rows // tm,),
        in_specs=in_specs,
        out_specs=out_spec,
        out_shape=out_shape,
        input_output_aliases=aliases,
        compiler_params=_params("arbitrary"),
        name="norm_matmul",
    )(*args)


def _kv_kernel(x_ref, g_ref, w_ref, stacked_ref, o_heads_ref, o_flat_ref):
    del stacked_ref
    n_heads, dh = o_heads_ref.shape[1:]
    a = _rms(x_ref[...], g_ref[...]).astype(BF16)
    y = jnp.dot(a, w_ref[...], preferred_element_type=F32)
    o_flat_ref[...] = y.astype(o_flat_ref.dtype)
    for h in range(n_heads):
        o_heads_ref[:, h, :] = y[:, h * dh:(h + 1) * dh]


def _kv_project(x, g, w, stacked, *, layer, tm, col_block):
    rows, d = x.shape
    n_heads, dh = stacked.shape[2:]
    return pl.pallas_call(
        _kv_kernel,
        grid=(rows // tm,),
        in_specs=[
            pl.BlockSpec((tm, d), lambda i: (i, 0)),
            pl.BlockSpec((1, d), lambda i: (0, 0)),
            pl.BlockSpec((None, d, d), lambda i: (layer, 0, col_block)),
            pl.BlockSpec(memory_space=pl.ANY),
        ],
        out_specs=[pl.BlockSpec((None, tm, n_heads, dh), lambda i: (layer, i, 0, 0)),
                   pl.BlockSpec((tm, d), lambda i: (i, 0))],
        out_shape=[jax.ShapeDtypeStruct(stacked.shape, stacked.dtype),
                   jax.ShapeDtypeStruct((rows, d), BF16)],
        input_output_aliases={3: 0},
        compiler_params=_params("arbitrary"),
        name="kv_project",
    )(x, g, w, stacked)


def _mnr_kernel(y_ref, w_ref, g_ref, res_ref, o_ref):
    z = jnp.dot(y_ref[...].astype(BF16), w_ref[...], preferred_element_type=F32)
    o_ref[...] = res_ref[...] + _rms(z, g_ref[...])


def _matmul_norm_residual(y, w, g, res, *, layer, tm):
    rows, d = res.shape
    k = w.shape[1]
    return pl.pallas_call(
        _mnr_kernel,
        grid=(rows // tm,),
        in_specs=[
            pl.BlockSpec((tm, k), lambda i: (i, 0)),
            pl.BlockSpec((None, k, d), lambda i: (layer, 0, 0)),
            pl.BlockSpec((1, d), lambda i: (0, 0)),
            pl.BlockSpec((tm, d), lambda i: (i, 0)),
        ],
        out_specs=pl.BlockSpec((tm, d), lambda i: (i, 0)),
        out_shape=jax.ShapeDtypeStruct((rows, d), F32),
        compiler_params=_params("arbitrary"),
        name="matmul_norm_residual",
    )(y, w, g, res)


def _cumsum_kernel(lf_ref, o_ref, carry_ref, *, n_heads):
    tc = lf_ref.shape[0]

    @pl.when(pl.program_id(1) == 0)
    def _():
        carry_ref[...] = jnp.zeros_like(carry_ref)

    lft = lf_ref[...].T
    upper = (lax.broadcasted_iota(jnp.int32, (tc, tc), 0)
             <= lax.broadcasted_iota(jnp.int32, (tc, tc), 1)).astype(F32)
    c = jnp.dot(lft, upper, precision=lax.Precision.HIGHEST,
                preferred_element_type=F32) + carry_ref[...]
    carry_ref[...] = c[:, tc - 1:tc]
    o_ref[0] = c[:n_heads] * LOG2_E


def _cumsum_heads(lf, *, layer, batch, seq, n_heads, tc):
    lanes = lf.shape[2]
    nt = seq // tc
    return pl.pallas_call(
        functools.partial(_cumsum_kernel, n_heads=n_heads),
        grid=(batch, nt),
        in_specs=[pl.BlockSpec((None, tc, lanes), lambda b, t: (layer, b * nt + t, 0))],
        out_specs=pl.BlockSpec((1, n_heads, tc), lambda b, t: (b, 0, t)),
        out_shape=jax.ShapeDtypeStruct((batch, n_heads, seq), F32),
        scratch_shapes=[pltpu.VMEM((lanes, 1), F32)],
        compiler_params=_params("arbitrary", "arbitrary"),
        name="logf_cumsum",
    )(lf)


def _attn_kernel(q_ref, k_ref, v_ref, cs_ref, o_ref, *, tile):
    seq = q_ref.shape[0]
    h = pl.program_id(1)
    cs = cs_ref[0, pl.ds(h, 1), :]
    row = lax.broadcasted_iota(jnp.int32, (tile, tile), 0)
    col = lax.broadcasted_iota(jnp.int32, (tile, tile), 1)
    causal = col <= row
    for qi in range(seq // tile):
        lo, hi = qi * tile, (qi + 1) * tile
        s = lax.dot_general(q_ref[lo:hi, :], k_ref[:hi, :], NT_DIMS,
                            preferred_element_type=F32) - cs[:, :hi]
        s_diag = jnp.where(causal, s[:, lo:hi], MASK_VALUE)
        s = s_diag if qi == 0 else jnp.concatenate([s[:, :lo], s_diag], axis=1)
        m = jnp.max(s, axis=-1, keepdims=True)
        p = jnp.exp2(s - m)
        l = jnp.sum(p, axis=-1, keepdims=True)
        o = jnp.dot(p.astype(BF16), v_ref[:hi, :], preferred_element_type=F32)
        o_ref[lo:hi, :] = (o / l).astype(o_ref.dtype)


def _attend_prompt(q, k, v, cs, *, batch, seq, n_heads, tile):
    rows, d = q.shape
    dh = d // n_heads
    return pl.pallas_call(
        functools.partial(_attn_kernel, tile=tile),
        grid=(batch, n_heads),
        in_specs=[
            pl.BlockSpec((seq, dh), lambda b, h: (b, h)),
            pl.BlockSpec((seq, dh), lambda b, h: (b, h)),
            pl.BlockSpec((seq, dh), lambda b, h: (b, h)),
            pl.BlockSpec((1, n_heads, seq), lambda b, h: (b, 0, 0)),
        ],
        out_specs=pl.BlockSpec((seq, dh), lambda b, h: (b, h)),
        out_shape=jax.ShapeDtypeStruct((rows, d), BF16),
        compiler_params=_params("arbitrary", "arbitrary"),
        name="fox_attend_prompt",
    )(q, k, v, cs)


def _conv_mix(u, um1, um2, b, cw):
    return b * (cw[0:1] * um2 + cw[1:2] * um1 + cw[2:3] * u)


def _conv_prompt_kernel(x_ref, g_ref, wb_ref, wc_ref, wh_ref, cw_ref, st_ref,
                        y_ref, tail_ref, a_ref, carry_ref, *, tiles_per_seq):
    tm, tn = y_ref.shape
    i = pl.program_id(0)
    j = pl.program_id(1)

    @pl.when(j == 0)
    def _():
        a_ref[...] = _rms(x_ref[...], g_ref[...]).astype(BF16)

    a = a_ref[...]
    b = jnp.dot(a, wb_ref[...], preferred_element_type=F32)
    c = jnp.dot(a, wc_ref[...], preferred_element_type=F32)
    hh = jnp.dot(a, wh_ref[...], preferred_element_type=F32)
    u = c * hh

    @pl.when(i % tiles_per_seq == 0)
    def _():
        carry_ref[j] = st_ref[0]

    prev2 = carry_ref[j, V7X_SUBLANES - 2:V7X_SUBLANES - 1, :]
    prev1 = carry_ref[j, V7X_SUBLANES - 1:V7X_SUBLANES, :]
    row = lax.broadcasted_iota(jnp.int32, (tm, tn), 0)
    um1 = jnp.where(row == 0, prev1, pltpu.roll(u, 1, 0))
    um2 = jnp.where(row == 0, prev2, jnp.where(row == 1, prev1, pltpu.roll(u, 2, 0)))
    y_ref[...] = _conv_mix(u, um1, um2, b, cw_ref[...]).astype(y_ref.dtype)
    tail = u[tm - V7X_SUBLANES:tm, :]
    carry_ref[j] = tail
    tail_ref[0, j] = tail


def _conv_prompt(x, g, w_in, conv_w, state8, *, layer, batch, seq, tm, tn):
    rows, d = x.shape
    nj = d // tn
    tiles_per_seq = seq // tm
    return pl.pallas_call(
        functools.partial(_conv_prompt_kernel, tiles_per_seq=tiles_per_seq),
        grid=(rows // tm, nj),
        in_specs=[
            pl.BlockSpec((tm, d), lambda i, j: (i, 0)),
            pl.BlockSpec((1, d), lambda i, j: (0, 0)),
            pl.BlockSpec((None, d, tn), lambda i, j: (layer, 0, j)),
            pl.BlockSpec((None, d, tn), lambda i, j: (layer, 0, j + nj)),
            pl.BlockSpec((None, d, tn), lambda i, j: (layer, 0, j + 2 * nj)),
            pl.BlockSpec((conv_w.shape[0], tn), lambda i, j: (0, j)),
            pl.BlockSpec((1, V7X_SUBLANES, tn), lambda i, j: (i // tiles_per_seq, 0, j)),
        ],
        out_specs=[
            pl.BlockSpec((tm, tn), lambda i, j: (i, j)),
            pl.BlockSpec((1, nj, V7X_SUBLANES, tn), lambda i, j: (i // tiles_per_seq, 0, 0, 0)),
        ],
        out_shape=[jax.ShapeDtypeStruct((rows, d), BF16),
                   jax.ShapeDtypeStruct((batch, nj, V7X_SUBLANES, tn), F32)],
        scratch_shapes=[pltpu.VMEM((tm, d), BF16),
                        pltpu.VMEM((nj, V7X_SUBLANES, tn), F32)],
        compiler_params=_params("arbitrary", "arbitrary"),
        name="conv_mix_prompt",
    )(x, g, w_in, w_in, w_in, conv_w, state8)


def _conv_sample_kernel(x_ref, g_ref, wb_ref, wc_ref, wh_ref, cw_ref, init1_ref, init2_ref,
                        y_ref, u_ref, *, seq):
    rows, tn = y_ref.shape
    a = _rms(x_ref[...], g_ref[...]).astype(BF16)
    b = jnp.dot(a, wb_ref[...], preferred_element_type=F32)
    c = jnp.dot(a, wc_ref[...], preferred_element_type=F32)
    hh = jnp.dot(a, wh_ref[...], preferred_element_type=F32)
    u = c * hh
    pos = lax.rem(lax.broadcasted_iota(jnp.int32, (rows, tn), 0), seq)
    um1 = jnp.where(pos >= 1, pltpu.roll(u, 1, 0), init1_ref[...])
    um2 = jnp.where(pos >= 2, pltpu.roll(u, 2, 0), init2_ref[...])
    y_ref[...] = _conv_mix(u, um1, um2, b, cw_ref[...]).astype(y_ref.dtype)
    u_ref[...] = u


def _conv_sample(x, g, w_in, conv_w, init1, init2, *, layer, seq, tn):
    rows, d = x.shape
    nj = d // tn
    return pl.pallas_call(
        functools.partial(_conv_sample_kernel, seq=seq),
        grid=(nj,),
        in_specs=[
            pl.BlockSpec((rows, d), lambda j: (0, 0)),
            pl.BlockSpec((1, d), lambda j: (0, 0)),
            pl.BlockSpec((None, d, tn), lambda j: (layer, 0, j)),
            pl.BlockSpec((None, d, tn), lambda j: (layer, 0, j + nj)),
            pl.BlockSpec((None, d, tn), lambda j: (layer, 0, j + 2 * nj)),
            pl.BlockSpec((conv_w.shape[0], tn), lambda j: (0, j)),
            pl.BlockSpec((rows, tn), lambda j: (0, j)),
            pl.BlockSpec((rows, tn), lambda j: (0, j)),
        ],
        out_specs=[pl.BlockSpec((rows, tn), lambda j: (0, j)),
                   pl.BlockSpec((rows, tn), lambda j: (0, j))],
        out_shape=[jax.ShapeDtypeStruct((rows, d), BF16),
                   jax.ShapeDtypeStruct((rows, d), F32)],
        compiler_params=_params("arbitrary"),
        name="conv_mix_sample",
    )(x, g, w_in, w_in, w_in, conv_w, init1, init2)


def _packed_cumsum(lf, carry, n_heads):
    rows, lanes = lf.shape
    lane = lax.broadcasted_iota(jnp.int32, lf.shape, 1)
    x = lf
    shift = n_heads
    while shift < lanes:
        x = x + jnp.where(lane >= shift, pltpu.roll(x, shift, 1), 0.0)
        shift *= 2
    tot = jnp.where(lane >= lanes - n_heads, x, 0.0)
    shift = n_heads
    while shift < lanes:
        tot = tot + pltpu.roll(tot, shift, 1)
        shift *= 2
    incl = tot
    if rows > 1:
        row = lax.broadcasted_iota(jnp.int32, lf.shape, 0)
        shift = 1
        while shift < rows:
            incl = incl + jnp.where(row >= shift, pltpu.roll(incl, shift, 0), 0.0)
            shift *= 2
    cum = x + (incl - tot) + carry
    return cum, incl[rows - 1:rows, :] + carry


def _decode_kernel(pt_ref, q_ref, kn_ref, vn_ref, lfn_ref, *rest, n_heads, pages_per_step):
    del pt_ref
    pp = pages_per_step
    k_refs, v_refs, lf_refs = rest[:pp], rest[pp:2 * pp], rest[2 * pp:3 * pp]
    o_ref, m_ref, l_ref, acc_ref, carry_ref = rest[3 * pp:]
    n_rows, dh = q_ref.shape[1], q_ref.shape[2]
    lanes = lf_refs[0].shape[3]
    p_idx = pl.program_id(1)

    @pl.when(p_idx == 0)
    def _():
        m_ref[...] = jnp.full_like(m_ref, MASK_VALUE)
        l_ref[...] = jnp.zeros_like(l_ref)
        acc_ref[...] = jnp.zeros_like(acc_ref)
        carry_ref[...] = jnp.zeros_like(carry_ref)

    row = lax.broadcasted_iota(jnp.int32, (n_rows, lanes), 0)
    lane = lax.broadcasted_iota(jnp.int32, (n_rows, lanes), 1)
    same_head = lax.rem(row, n_heads) == lax.rem(lane, n_heads)

    def process(k_blocks, v_blocks, lf, valid):
        q = q_ref[0]
        cum, carry = _packed_cumsum(lf, carry_ref[...], n_heads)
        carry_ref[...] = carry
        parts = []
        c_row = 0
        for kb in k_blocks:
            s = lax.dot_general(q, kb, NT_DIMS, preferred_element_type=F32)
            for j in range(kb.shape[0] // lanes):
                sj = s[:, j * lanes:(j + 1) * lanes] - cum[c_row:c_row + 1, :]
                parts.append(jnp.where(valid, sj, MASK_VALUE))
                c_row += 1
        s = jnp.concatenate(parts, axis=1)
        m_old = m_ref[...]
        m_new = jnp.maximum(m_old, jnp.max(s, axis=-1, keepdims=True))
        alpha = jnp.exp(m_old - m_new)
        p = jnp.exp(s - m_new)
        l_ref[...] = alpha * l_ref[...] + jnp.sum(p, axis=-1, keepdims=True)
        pv = None
        col = 0
        for vb in v_blocks:
            part = jnp.dot(p[:, col:col + vb.shape[0]], vb, preferred_element_type=F32)
            pv = part if pv is None else pv + part
            col += vb.shape[0]
        acc_ref[...] = alpha * acc_ref[...] + pv
        m_ref[...] = m_new

    process([r[0, 0] for r in k_refs], [r[0, 0] for r in v_refs],
            jnp.concatenate([r[0, 0] for r in lf_refs], axis=0), same_head)

    @pl.when(p_idx == pl.num_programs(1) - 1)
    def _():
        causal = (lane // n_heads) <= (row // n_heads)
        process([kn_ref[0]], [vn_ref[0]], lfn_ref[0], same_head & causal)
        o_ref[0] = acc_ref[...] / l_ref[...]


def _attend_sample(q, k_new, v_new, lf_new, cache_k, cache_v, cache_lf, page_table, *,
                   layer, n_heads, pages_per_step):
    n_b, n_rows, dh = q.shape
    page_rows = cache_k.shape[2]
    lf_rows, lanes = cache_lf.shape[2:]
    n_pages = page_table.shape[1]
    pp = pages_per_step
    n_steps = n_pages // pp

    def page_spec(rows, cols, r):
        return pl.BlockSpec((1, 1, rows, cols),
                            lambda b, p, pt: (layer, pt[b, p * pp + r], 0, 0))

    in_specs = [
        pl.BlockSpec((1, n_rows, dh), lambda b, p, pt: (b, 0, 0)),
        pl.BlockSpec((1, lanes, dh), lambda b, p, pt: (b, 0, 0)),
        pl.BlockSpec((1, lanes, dh), lambda b, p, pt: (b, 0, 0)),
        pl.BlockSpec((1, 1, lanes), lambda b, p, pt: (b, 0, 0)),
    ]
    in_specs += [page_spec(page_rows, dh, r) for r in range(pp)]
    in_specs += [page_spec(page_rows, dh, r) for r in range(pp)]
    in_specs += [page_spec(lf_rows, lanes, r) for r in range(pp)]
    grid_spec = pltpu.PrefetchScalarGridSpec(
        num_scalar_prefetch=1,
        grid=(n_b, n_steps),
        in_specs=in_specs,
        out_specs=pl.BlockSpec((1, n_rows, dh), lambda b, p, pt: (b, 0, 0)),
        scratch_shapes=[
            pltpu.VMEM((n_rows, 1), F32),
            pltpu.VMEM((n_rows, 1), F32),
            pltpu.VMEM((n_rows, dh), F32),
            pltpu.VMEM((1, lanes), F32),
        ],
    )
    return pl.pallas_call(
        functools.partial(_decode_kernel, n_heads=n_heads, pages_per_step=pp),
        grid_spec=grid_spec,
        out_shape=jax.ShapeDtypeStruct((n_b, n_rows, dh), F32),
        compiler_params=_params("arbitrary", "arbitrary"),
        name="fox_attend_sample",
    )(page_table, q, k_new, v_new, lf_new,
      *([cache_k] * pp), *([cache_v] * pp), *([cache_lf] * pp))


def kernel(x_prompt, x_sample, cache_k, cache_v, cache_logf, state_conv, page_table, norm_g,
           w_ffn_up, w_ffn_down, w_fox_in, b_fox_f, w_fox_out, w_conv_in, w_conv_w, w_conv_out):
    n_b, seq, d = x_prompt.shape
    n_db, n_tok, _ = x_sample.shape
    depth = norm_g.shape[0]
    n_fox = w_fox_in.shape[0]
    n_heads = b_fox_f.shape[1]
    dh = d // n_heads
    page = cache_k.shape[2]
    attn_scale = dh ** -0.5
    conv_width = w_conv_w.shape[1]

    tm_p = 512
    rows_p = n_b * seq
    rows_s = n_db * n_tok

    hp = x_prompt.reshape(rows_p, d)
    hs = x_sample.reshape(rows_s, d)

    keys_per_row = V7X_LANES // n_heads
    n_pool = cache_k.shape[1]
    cache_k2 = cache_k.reshape(n_fox, n_pool, page * n_heads, dh)
    cache_v2 = cache_v.reshape(n_fox, n_pool, page * n_heads, dh)
    cache_lf = cache_logf.reshape(n_fox, n_pool, page // keys_per_row, V7X_LANES)

    w_up_bf = w_ffn_up.astype(BF16)
    w_down_bf = w_ffn_down.astype(BF16)
    w_fox_in_bf = w_fox_in.astype(BF16)
    w_fox_out_bf = w_fox_out.astype(BF16)
    w_conv_in_bf = w_conv_in.astype(BF16)
    w_conv_out_bf = w_conv_out.astype(BF16)
    w_fgate = jnp.pad(w_fox_in[:, :, 3 * d:], ((0, 0), (0, 0), (0, V7X_LANES - n_heads)))
    w_fgate = w_fgate.astype(BF16)
    b_fgate = jnp.pad(b_fox_f, ((0, 0), (0, V7X_LANES - n_heads)))[:, None, :]

    k_all = jnp.zeros((n_fox, rows_p, n_heads, dh), F32)
    v_all = jnp.zeros((n_fox, rows_p, n_heads, dh), F32)
    lf_all = jnp.zeros((n_fox, rows_p, V7X_LANES), F32)
    ksl, vsl, fsl, cp, csl = [], [], [], [], []
    for i in range(depth):
        g = norm_g[i][:, None, :]
        mi = i // 2

        hp = _ffn_half(hp, g[0], g[1], w_up_bf, w_down_bf, layer=i, half=0, tm=tm_p, tf=512)
        hs = _ffn_half(hs, g[0], g[1], w_up_bf, w_down_bf, layer=i, half=0, tm=rows_s, tf=1024)

        if i % 2 == 0:
            bf = b_fgate[mi]

            q = _norm_matmul(hp, g[2], w_fox_in_bf, layer=mi, tm=tm_p, n=d, col_block=0,
                             out_dtype=BF16, scale=attn_scale * LOG2_E)
            k_all, k_bf = _kv_project(hp, g[2], w_fox_in_bf, k_all, layer=mi, tm=tm_p,
                                      col_block=1)
            v_all, v_bf = _kv_project(hp, g[2], w_fox_in_bf, v_all, layer=mi, tm=tm_p,
                                      col_block=2)
            lf_all = _norm_matmul(hp, g[2], w_fgate, layer=mi, tm=tm_p, n=V7X_LANES, bias=bf,
                                  stack=(mi, lf_all))
            cs = _cumsum_heads(lf_all, layer=mi, batch=n_b, seq=seq, n_heads=n_heads, tc=512)
            o = _attend_prompt(q, k_bf, v_bf, cs, batch=n_b, seq=seq, n_heads=n_heads,
                               tile=256)
            hp = _matmul_norm_residual(o, w_fox_out_bf, g[3], hp, layer=mi, tm=tm_p)

            q = _norm_matmul(hs, g[2], w_fox_in_bf, layer=mi, tm=rows_s, n=d, col_block=0,
                             scale=attn_scale)
            k = _norm_matmul(hs, g[2], w_fox_in_bf, layer=mi, tm=rows_s, n=d, col_block=1)
            v = _norm_matmul(hs, g[2], w_fox_in_bf, layer=mi, tm=rows_s, n=d, col_block=2)
            lf = _norm_matmul(hs, g[2], w_fgate, layer=mi, tm=rows_s, n=V7X_LANES, bias=bf)
            pad = ((0, 0), (0, keys_per_row - n_tok), (0, 0), (0, 0))
            k_new = jnp.pad(k.reshape(n_db, n_tok, n_heads, dh), pad)
            v_new = jnp.pad(v.reshape(n_db, n_tok, n_heads, dh), pad)
            lf_new = lf[:, :n_heads].reshape(n_db, n_tok, n_heads)
            o = _attend_sample(q.reshape(n_db, n_tok * n_heads, dh),
                               k_new.reshape(n_db, V7X_LANES, dh),
                               v_new.reshape(n_db, V7X_LANES, dh),
                               jnp.pad(lf_new, pad[:3]).reshape(n_db, 1, V7X_LANES),
                               cache_k2, cache_v2, cache_lf, page_table,
                               layer=mi, n_heads=n_heads, pages_per_step=8)
            hs = _matmul_norm_residual(o.reshape(rows_s, d), w_fox_out_bf, g[3], hs,
                                       layer=mi, tm=rows_s)
            ksl.append(k.reshape(n_db, n_tok, n_heads, dh))
            vsl.append(v.reshape(n_db, n_tok, n_heads, dh))
            fsl.append(lf_new)
        else:
            cw = w_conv_w[mi]

            state8 = jnp.zeros((n_b, V7X_SUBLANES, d), F32)
            y, tail = _conv_prompt(hp, g[2], w_conv_in_bf, cw, state8, layer=mi, batch=n_b,
                                   seq=seq, tm=tm_p, tn=512)
            hp = _matmul_norm_residual(y, w_conv_out_bf, g[3], hp, layer=mi, tm=tm_p)
            tail = jnp.swapaxes(tail[:, :, V7X_SUBLANES - (conv_width - 1):], 1, 2)
            cp.append(tail.reshape(n_b, conv_width - 1, d))

            st = state_conv[mi]
            zero = jnp.zeros((n_db, n_tok, d), F32)
            init1 = zero.at[:, 0].set(st[:, 1]).reshape(rows_s, d)
            init2 = zero.at[:, 0].set(st[:, 0]).at[:, 1].set(st[:, 1]).reshape(rows_s, d)
            y, u = _conv_sample(hs, g[2], w_conv_in_bf, cw, init1, init2, layer=mi,
                                seq=n_tok, tn=512)
            hs = _matmul_norm_residual(y, w_conv_out_bf, g[3], hs, layer=mi, tm=rows_s)
            csl.append(u.reshape(n_db, n_tok, d)[:, n_tok - (conv_width - 1):])

        hp = _ffn_half(hp, g[4], g[5], w_up_bf, w_down_bf, layer=i, half=1, tm=tm_p, tf=512)
        hs = _ffn_half(hs, g[4], g[5], w_up_bf, w_down_bf, layer=i, half=1, tm=rows_s, tf=1024)

    return (hp.reshape(n_b, seq, d), hs.reshape(n_db, n_tok, d),
            k_all.reshape(n_fox, n_b, seq, n_heads, dh),
            v_all.reshape(n_fox, n_b, seq, n_heads, dh),
            lf_all[:, :, :n_heads].reshape(n_fox, n_b, seq, n_heads),
            jnp.stack(cp), jnp.stack(ksl), jnp.stack(vsl), jnp.stack(fsl), jnp.stack(csl))
```

```python
import functools
import math

import jax
import jax.numpy as jnp
from jax import lax
from jax.experimental import pallas as pl
from jax.experimental.pallas import tpu as pltpu

F32 = jnp.float32
BF16 = jnp.bfloat16

NORM_EPS = 1e-6
MACARON_SCALE = 0.5
MASK_VALUE = -1e30
LOG2_E = math.log2(math.e)
V7X_LANES = 128
V7X_SUBLANES = 8
V7X_VMEM_LIMIT_BYTES = 56 * 1024 * 1024
NT_DIMS = (((1,), (1,)), ((), ()))


def _params(*semantics):
    return pltpu.CompilerParams(dimension_semantics=semantics,
                                vmem_limit_bytes=V7X_VMEM_LIMIT_BYTES)


def _rms(x, g):
    return x * lax.rsqrt(jnp.mean(x * x, axis=-1, keepdims=True) + NORM_EPS) * g


def _ffn_kernel(x_ref, gpre_ref, gpost_ref, wg_ref, wu_ref, wd_ref, *rest, cast_next):
    j = pl.program_id(1)
    last = pl.num_programs(1) - 1
    if cast_next:
        next_up_ref, next_down_ref, o_ref, up_bf_ref, down_bf_ref, a_ref, acc_ref = rest
    else:
        o_ref, a_ref, acc_ref = rest

    def partial_out(a):
        if cast_next:
            up_bf_ref[...] = next_up_ref[...].astype(BF16)
            down_bf_ref[...] = next_down_ref[...].astype(BF16)
        gate = jnp.dot(a, wg_ref[...], preferred_element_type=F32)
        up = jnp.dot(a, wu_ref[...], preferred_element_type=F32)
        h = (gate * jax.nn.sigmoid(gate) * up).astype(BF16)
        return jnp.dot(h, wd_ref[...], preferred_element_type=F32)

    @pl.when(j == 0)
    def _():
        a = _rms(x_ref[...], gpre_ref[...]).astype(BF16)
        a_ref[...] = a
        acc_ref[...] = partial_out(a)

    @pl.when(jnp.logical_and(j > 0, j < last))
    def _():
        acc_ref[...] += partial_out(a_ref[...])

    @pl.when(j == last)
    def _():
        y = acc_ref[...] + partial_out(a_ref[...])
        o_ref[...] = x_ref[...] + MACARON_SCALE * _rms(y, gpost_ref[...])


def _ffn_half(x, g_pre, g_post, w_up, w_down, *, tm, tf, cast_next=None):
    rows, d = x.shape
    dff = w_down.shape[0]
    nj = dff // tf
    assert nj >= 2
    in_specs = [
        pl.BlockSpec((tm, d), lambda i, j: (i, 0)),
        pl.BlockSpec((1, d), lambda i, j: (0, 0)),
        pl.BlockSpec((1, d), lambda i, j: (0, 0)),
        pl.BlockSpec((d, tf), lambda i, j: (0, j)),
        pl.BlockSpec((d, tf), lambda i, j: (0, j + nj)),
        pl.BlockSpec((tf, d), lambda i, j: (j, 0)),
    ]
    args = [x, g_pre, g_post, w_up, w_up, w_down]
    out_specs = [pl.BlockSpec((tm, d), lambda i, j: (i, 0))]
    out_shape = [jax.ShapeDtypeStruct((rows, d), F32)]
    if cast_next is not None:
        up_f32, down_f32, layer, half = cast_next
        n_steps = (rows // tm) * nj
        up_rows, down_rows = d // n_steps, dff // n_steps
        in_specs += [
            pl.BlockSpec((None, None, up_rows, 2 * dff), lambda i, j: (layer, half, i * nj + j, 0)),
            pl.BlockSpec((None, None, down_rows, d), lambda i, j: (layer, half, i * nj + j, 0)),
        ]
        args += [up_f32, down_f32]
        out_specs += [pl.BlockSpec((up_rows, 2 * dff), lambda i, j: (i * nj + j, 0)),
                      pl.BlockSpec((down_rows, d), lambda i, j: (i * nj + j, 0))]
        out_shape += [jax.ShapeDtypeStruct((d, 2 * dff), BF16),
                      jax.ShapeDtypeStruct((dff, d), BF16)]
    return pl.pallas_call(
        functools.partial(_ffn_kernel, cast_next=cast_next is not None),
        grid=(rows // tm, nj),
        in_specs=in_specs,
        out_specs=out_specs,
        out_shape=out_shape,
        scratch_shapes=[pltpu.VMEM((tm, d), BF16), pltpu.VMEM((tm, d), F32)],
        compiler_params=_params("arbitrary", "arbitrary"),
        name="ffn_half",
    )(*args)


def _nm_kernel(x_ref, g_ref, w_ref, b_ref, *rest, scale, log_sigmoid):
    o_ref = rest[-1]
    a = _rms(x_ref[...], g_ref[...]).astype(BF16)
    y = jnp.dot(a, w_ref[...], preferred_element_type=F32)
    if log_sigmoid:
        y = jax.nn.log_sigmoid(y + b_ref[...])
    if scale is not None:
        y = y * scale
    o_ref[...] = y.astype(o_ref.dtype)


def _norm_matmul(x, g, w, *, layer, tm, n, col_block=0, out_dtype=F32, scale=None, bias=None,
                 stack=None):
    rows, d = x.shape
    log_sigmoid = bias is not None
    if bias is None:
        bias = jnp.zeros((1, n), F32)
    in_specs = [
        pl.BlockSpec((tm, d), lambda i: (i, 0)),
        pl.BlockSpec((1, d), lambda i: (0, 0)),
        pl.BlockSpec((None, d, n), lambda i: (layer, 0, col_block)),
        pl.BlockSpec((1, n), lambda i: (0, 0)),
    ]
    args = [x, g, w, bias]
    aliases = {}
    if stack is None:
        out_spec = pl.BlockSpec((tm, n), lambda i: (i, 0))
        out_shape = jax.ShapeDtypeStruct((rows, n), out_dtype)
    else:
        slab, stacked = stack
        out_spec = pl.BlockSpec((None, tm, n), lambda i: (slab, i, 0))
        out_shape = jax.ShapeDtypeStruct(stacked.shape, stacked.dtype)
        in_specs.append(pl.BlockSpec(memory_space=pl.ANY))
        args.append(stacked)
        aliases = {len(args) - 1: 0}
    return pl.pallas_call(
        functools.partial(_nm_kernel, scale=scale, log_sigmoid=log_sigmoid),
        grid=(rows // tm,),
        in_specs=in_specs,
        out_specs=out_spec,
        out_shape=out_shape,
        input_output_aliases=aliases,
        compiler_params=_params("arbitrary"),
        name="norm_matmul",
    )(*args)


def _kv_kernel(x_ref, g_ref, w_ref, stacked_ref, o_heads_ref, o_flat_ref):
    del stacked_ref
    n_heads, dh = o_heads_ref.shape[1:]
    a = _rms(x_ref[...], g_ref[...]).astype(BF16)
    y = jnp.dot(a, w_ref[...], preferred_element_type=F32)
    o_flat_ref[...] = y.astype(o_flat_ref.dtype)
    for h in range(n_heads):
        o_heads_ref[:, h, :] = y[:, h * dh:(h + 1) * dh]


def _kv_project(x, g, w, stacked, *, layer, tm, col_block):
    rows, d = x.shape
    n_heads, dh = stacked.shape[2:]
    return pl.pallas_call(
        _kv_kernel,
        grid=(rows // tm,),
        in_specs=[
            pl.BlockSpec((tm, d), lambda i: (i, 0)),
            pl.BlockSpec((1, d), lambda i: (0, 0)),
            pl.BlockSpec((None, d, d), lambda i: (layer, 0, col_block)),
            pl.BlockSpec(memory_space=pl.ANY),
        ],
        out_specs=[pl.BlockSpec((None, tm, n_heads, dh), lambda i: (layer, i, 0, 0)),
                   pl.BlockSpec((tm, d), lambda i: (i, 0))],
        out_shape=[jax.ShapeDtypeStruct(stacked.shape, stacked.dtype),
                   jax.ShapeDtypeStruct((rows, d), BF16)],
        input_output_aliases={3: 0},
        compiler_params=_params("arbitrary"),
        name="kv_project",
    )(x, g, w, stacked)


def _mnr_kernel(y_ref, w_ref, g_ref, res_ref, o_ref):
    z = jnp.dot(y_ref[...].astype(BF16), w_ref[...], preferred_element_type=F32)
    o_ref[...] = res_ref[...] + _rms(z, g_ref[...])


def _matmul_norm_residual(y, w, g, res, *, layer, tm):
    rows, d = res.shape
    k = w.shape[1]
    return pl.pallas_call(
        _mnr_kernel,
        grid=(rows // tm,),
        in_specs=[
            pl.BlockSpec((tm, k), lambda i: (i, 0)),
            pl.BlockSpec((None, k, d), lambda i: (layer, 0, 0)),
            pl.BlockSpec((1, d), lambda i: (0, 0)),
            pl.BlockSpec((tm, d), lambda i: (i, 0)),
        ],
        out_specs=pl.BlockSpec((tm, d), lambda i: (i, 0)),
        out_shape=jax.ShapeDtypeStruct((rows, d), F32),
        compiler_params=_params("arbitrary"),
        name="matmul_norm_residual",
    )(y, w, g, res)


def _cumsum_kernel(lf_ref, o_ref, carry_ref, *, n_heads):
    tc = lf_ref.shape[0]

    @pl.when(pl.program_id(1) == 0)
    def _():
        carry_ref[...] = jnp.zeros_like(carry_ref)

    lft = lf_ref[...].T
    upper = (lax.broadcasted_iota(jnp.int32, (tc, tc), 0)
             <= lax.broadcasted_iota(jnp.int32, (tc, tc), 1)).astype(F32)
    c = jnp.dot(lft, upper, precision=lax.Precision.HIGHEST,
                preferred_element_type=F32) + carry_ref[...]
    carry_ref[...] = c[:, tc - 1:tc]
    o_ref[0] = c[:n_heads] * LOG2_E


def _cumsum_heads(lf, *, layer, batch, seq, n_heads, tc):
    lanes = lf.shape[2]
    nt = seq // tc
    return pl.pallas_call(
        functools.partial(_cumsum_kernel, n_heads=n_heads),
        grid=(batch, nt),
        in_specs=[pl.BlockSpec((None, tc, lanes), lambda b, t: (layer, b * nt + t, 0))],
        out_specs=pl.BlockSpec((1, n_heads, tc), lambda b, t: (b, 0, t)),
        out_shape=jax.ShapeDtypeStruct((batch, n_heads, seq), F32),
        scratch_shapes=[pltpu.VMEM((lanes, 1), F32)],
        compiler_params=_params("arbitrary", "arbitrary"),
        name="logf_cumsum",
    )(lf)


def _attn_kernel(q_ref, k_ref, v_ref, cs_ref, o_ref, *, tile):
    seq, dh = q_ref.shape
    h = pl.program_id(1)
    cs = cs_ref[0, pl.ds(h, 1), :]
    row = lax.broadcasted_iota(jnp.int32, (tile, tile), 0)
    col = lax.broadcasted_iota(jnp.int32, (tile, tile), 1)
    causal = col <= row
    for qi in range(seq // tile):
        q = q_ref[qi * tile:(qi + 1) * tile, :]
        m = jnp.full((tile, 1), MASK_VALUE, F32)
        l = jnp.zeros((tile, 1), F32)
        acc = jnp.zeros((tile, dh), F32)
        for j in range(qi + 1):
            keys = slice(j * tile, (j + 1) * tile)
            s = lax.dot_general(q, k_ref[keys, :], NT_DIMS, preferred_element_type=F32)
            s = s - cs[:, keys]
            if j == qi:
                s = jnp.where(causal, s, MASK_VALUE)
            m_new = jnp.maximum(m, jnp.max(s, axis=-1, keepdims=True))
            alpha = jnp.exp2(m - m_new)
            p = jnp.exp2(s - m_new)
            l = alpha * l + jnp.sum(p, axis=-1, keepdims=True)
            acc = alpha * acc + jnp.dot(p.astype(BF16), v_ref[keys, :],
                                        preferred_element_type=F32)
            m = m_new
        o_ref[qi * tile:(qi + 1) * tile, :] = (acc / l).astype(o_ref.dtype)


def _attend_prompt(q, k, v, cs, *, batch, seq, n_heads, tile):
    rows, d = q.shape
    dh = d // n_heads
    return pl.pallas_call(
        functools.partial(_attn_kernel, tile=tile),
        grid=(batch, n_heads),
        in_specs=[
            pl.BlockSpec((seq, dh), lambda b, h: (b, h)),
            pl.BlockSpec((seq, dh), lambda b, h: (b, h)),
            pl.BlockSpec((seq, dh), lambda b, h: (b, h)),
            pl.BlockSpec((1, n_heads, seq), lambda b, h: (b, 0, 0)),
        ],
        out_specs=pl.BlockSpec((seq, dh), lambda b, h: (b, h)),
        out_shape=jax.ShapeDtypeStruct((rows, d), BF16),
        compiler_params=_params("arbitrary", "arbitrary"),
        name="fox_attend_prompt",
    )(q, k, v, cs)


def _conv_mix(u, um1, um2, b, cw):
    return b * (cw[0:1] * um2 + cw[1:2] * um1 + cw[2:3] * u)


def _conv_prompt_kernel(x_ref, g_ref, wb_ref, wc_ref, wh_ref, cw_ref, st_ref,
                        y_ref, tail_ref, a_ref, carry_ref, *, tiles_per_seq):
    tm, tn = y_ref.shape
    i = pl.program_id(0)
    j = pl.program_id(1)

    @pl.when(j == 0)
    def _():
        a_ref[...] = _rms(x_ref[...], g_ref[...]).astype(BF16)

    a = a_ref[...]
    b = jnp.dot(a, wb_ref[...], preferred_element_type=F32)
    c = jnp.dot(a, wc_ref[...], preferred_element_type=F32)
    hh = jnp.dot(a, wh_ref[...], preferred_element_type=F32)
    u = c * hh

    @pl.when(i % tiles_per_seq == 0)
    def _():
        carry_ref[j] = st_ref[0]

    prev2 = carry_ref[j, V7X_SUBLANES - 2:V7X_SUBLANES - 1, :]
    prev1 = carry_ref[j, V7X_SUBLANES - 1:V7X_SUBLANES, :]
    row = lax.broadcasted_iota(jnp.int32, (tm, tn), 0)
    um1 = jnp.where(row == 0, prev1, pltpu.roll(u, 1, 0))
    um2 = jnp.where(row == 0, prev2, jnp.where(row == 1, prev1, pltpu.roll(u, 2, 0)))
    y_ref[...] = _conv_mix(u, um1, um2, b, cw_ref[...]).astype(y_ref.dtype)
    tail = u[tm - V7X_SUBLANES:tm, :]
    carry_ref[j] = tail
    tail_ref[0, j] = tail


def _conv_prompt(x, g, w_in, conv_w, state8, *, layer, batch, seq, tm, tn):
    rows, d = x.shape
    nj = d // tn
    tiles_per_seq = seq // tm
    return pl.pallas_call(
        functools.partial(_conv_prompt_kernel, tiles_per_seq=tiles_per_seq),
        grid=(rows // tm, nj),
        in_specs=[
            pl.BlockSpec((tm, d), lambda i, j: (i, 0)),
            pl.BlockSpec((1, d), lambda i, j: (0, 0)),
            pl.BlockSpec((None, d, tn), lambda i, j: (layer, 0, j)),
            pl.BlockSpec((None, d, tn), lambda i, j: (layer, 0, j + nj)),
            pl.BlockSpec((None, d, tn), lambda i, j: (layer, 0, j + 2 * nj)),
            pl.BlockSpec((conv_w.shape[0], tn), lambda i, j: (0, j)),
            pl.BlockSpec((1, V7X_SUBLANES, tn), lambda i, j: (i // tiles_per_seq, 0, j)),
        ],
        out_specs=[
            pl.BlockSpec((tm, tn), lambda i, j: (i, j)),
            pl.BlockSpec((1, nj, V7X_SUBLANES, tn), lambda i, j: (i // tiles_per_seq, 0, 0, 0)),
        ],
        out_shape=[jax.ShapeDtypeStruct((rows, d), BF16),
                   jax.ShapeDtypeStruct((batch, nj, V7X_SUBLANES, tn), F32)],
        scratch_shapes=[pltpu.VMEM((tm, d), BF16),
                        pltpu.VMEM((nj, V7X_SUBLANES, tn), F32)],
        compiler_params=_params("arbitrary", "arbitrary"),
        name="conv_mix_prompt",
    )(x, g, w_in, w_in, w_in, conv_w, state8)


def _conv_sample_kernel(x_ref, g_ref, wb_ref, wc_ref, wh_ref, cw_ref, init1_ref, init2_ref,
                        y_ref, u_ref, *, seq):
    rows, tn = y_ref.shape
    a = _rms(x_ref[...], g_ref[...]).astype(BF16)
    b = jnp.dot(a, wb_ref[...], preferred_element_type=F32)
    c = jnp.dot(a, wc_ref[...], preferred_element_type=F32)
    hh = jnp.dot(a, wh_ref[...], preferred_element_type=F32)
    u = c * hh
    pos = lax.rem(lax.broadcasted_iota(jnp.int32, (rows, tn), 0), seq)
    um1 = jnp.where(pos >= 1, pltpu.roll(u, 1, 0), init1_ref[...])
    um2 = jnp.where(pos >= 2, pltpu.roll(u, 2, 0), init2_ref[...])
    y_ref[...] = _conv_mix(u, um1, um2, b, cw_ref[...]).astype(y_ref.dtype)
    u_ref[...] = u


def _conv_sample(x, g, w_in, conv_w, init1, init2, *, layer, seq, tn):
    rows, d = x.shape
    nj = d // tn
    return pl.pallas_call(
        functools.partial(_conv_sample_kernel, seq=seq),
        grid=(nj,),
        in_specs=[
            pl.BlockSpec((rows, d), lambda j: (0, 0)),
            pl.BlockSpec((1, d), lambda j: (0, 0)),
            pl.BlockSpec((None, d, tn), lambda j: (layer, 0, j)),
            pl.BlockSpec((None, d, tn), lambda j: (layer, 0, j + nj)),
            pl.BlockSpec((None, d, tn), lambda j: (layer, 0, j + 2 * nj)),
            pl.BlockSpec((conv_w.shape[0], tn), lambda j: (0, j)),
            pl.BlockSpec((rows, tn), lambda j: (0, j)),
            pl.BlockSpec((rows, tn), lambda j: (0, j)),
        ],
        out_specs=[pl.BlockSpec((rows, tn), lambda j: (0, j)),
                   pl.BlockSpec((rows, tn), lambda j: (0, j))],
        out_shape=[jax.ShapeDtypeStruct((rows, d), BF16),
                   jax.ShapeDtypeStruct((rows, d), F32)],
        compiler_params=_params("arbitrary"),
        name="conv_mix_sample",
    )(x, g, w_in, w_in, w_in, conv_w, init1, init2)


def _packed_cumsum(lf, carry, n_heads):
    rows, lanes = lf.shape
    lane = lax.broadcasted_iota(jnp.int32, lf.shape, 1)
    x = lf
    shift = n_heads
    while shift < lanes:
        x = x + jnp.where(lane >= shift, pltpu.roll(x, shift, 1), 0.0)
        shift *= 2
    tot = jnp.where(lane >= lanes - n_heads, x, 0.0)
    shift = n_heads
    while shift < lanes:
        tot = tot + pltpu.roll(tot, shift, 1)
        shift *= 2
    incl = tot
    if rows > 1:
        row = lax.broadcasted_iota(jnp.int32, lf.shape, 0)
        shift = 1
        while shift < rows:
            incl = incl + jnp.where(row >= shift, pltpu.roll(incl, shift, 0), 0.0)
            shift *= 2
    cum = x + (incl - tot) + carry
    return cum, incl[rows - 1:rows, :] + carry


def _decode_kernel(pt_ref, q_ref, kn_ref, vn_ref, lfn_ref, *rest, n_heads, pages_per_step):
    del pt_ref
    pp = pages_per_step
    k_refs, v_refs, lf_refs = rest[:pp], rest[pp:2 * pp], rest[2 * pp:3 * pp]
    o_ref, m_ref, l_ref, acc_ref, carry_ref = rest[3 * pp:]
    n_rows, dh = q_ref.shape[1], q_ref.shape[2]
    lanes = lf_refs[0].shape[3]
    p_idx = pl.program_id(1)

    @pl.when(p_idx == 0)
    def _():
        m_ref[...] = jnp.full_like(m_ref, MASK_VALUE)
        l_ref[...] = jnp.zeros_like(l_ref)
        acc_ref[...] = jnp.zeros_like(acc_ref)
        carry_ref[...] = jnp.zeros_like(carry_ref)

    row = lax.broadcasted_iota(jnp.int32, (n_rows, lanes), 0)
    lane = lax.broadcasted_iota(jnp.int32, (n_rows, lanes), 1)
    same_head = lax.rem(row, n_heads) == lax.rem(lane, n_heads)

    def process(k_blocks, v_blocks, lf, valid):
        q = q_ref[0]
        cum, carry = _packed_cumsum(lf, carry_ref[...], n_heads)
        carry_ref[...] = carry
        parts = []
        c_row = 0
        for kb in k_blocks:
            s = lax.dot_general(q, kb, NT_DIMS, preferred_element_type=F32)
            for j in range(kb.shape[0] // lanes):
                sj = s[:, j * lanes:(j + 1) * lanes] - cum[c_row:c_row + 1, :]
                parts.append(jnp.where(valid, sj, MASK_VALUE))
                c_row += 1
        s = jnp.concatenate(parts, axis=1)
        m_old = m_ref[...]
        m_new = jnp.maximum(m_old, jnp.max(s, axis=-1, keepdims=True))
        alpha = jnp.exp(m_old - m_new)
        p = jnp.exp(s - m_new)
        l_ref[...] = alpha * l_ref[...] + jnp.sum(p, axis=-1, keepdims=True)
        pv = None
        col = 0
        for vb in v_blocks:
            part = jnp.dot(p[:, col:col + vb.shape[0]], vb, preferred_element_type=F32)
            pv = part if pv is None else pv + part
            col += vb.shape[0]
        acc_ref[...] = alpha * acc_ref[...] + pv
        m_ref[...] = m_new

    process([r[0, 0] for r in k_refs], [r[0, 0] for r in v_refs],
            jnp.concatenate([r[0, 0] for r in lf_refs], axis=0), same_head)

    @pl.when(p_idx == pl.num_programs(1) - 1)
    def _():
        causal = (lane // n_heads) <= (row // n_heads)
        process([kn_ref[0]], [vn_ref[0]], lfn_ref[0], same_head & causal)
        o_ref[0] = acc_ref[...] / l_ref[...]


def _attend_sample(q, k_new, v_new, lf_new, cache_k, cache_v, cache_lf, page_table, *,
                   layer, n_heads, pages_per_step):
    n_b, n_rows, dh = q.shape
    page_rows = cache_k.shape[2]
    lf_rows, lanes = cache_lf.shape[2:]
    n_pages = page_table.shape[1]
    pp = pages_per_step
    n_steps = n_pages // pp

    def page_spec(rows, cols, r):
        return pl.BlockSpec((1, 1, rows, cols),
                            lambda b, p, pt: (layer, pt[b, p * pp + r], 0, 0))

    in_specs = [
        pl.BlockSpec((1, n_rows, dh), lambda b, p, pt: (b, 0, 0)),
        pl.BlockSpec((1, lanes, dh), lambda b, p, pt: (b, 0, 0)),
        pl.BlockSpec((1, lanes, dh), lambda b, p, pt: (b, 0, 0)),
        pl.BlockSpec((1, 1, lanes), lambda b, p, pt: (b, 0, 0)),
    ]
    in_specs += [page_spec(page_rows, dh, r) for r in range(pp)]
    in_specs += [page_spec(page_rows, dh, r) for r in range(pp)]
    in_specs += [page_spec(lf_rows, lanes, r) for r in range(pp)]
    grid_spec = pltpu.PrefetchScalarGridSpec(
        num_scalar_prefetch=1,
        grid=(n_b, n_steps),
        in_specs=in_specs,
        out_specs=pl.BlockSpec((1, n_rows, dh), lambda b, p, pt: (b, 0, 0)),
        scratch_shapes=[
            pltpu.VMEM((n_rows, 1), F32),
            pltpu.VMEM((n_rows, 1), F32),
            pltpu.VMEM((n_rows, dh), F32),
            pltpu.VMEM((1, lanes), F32),
        ],
    )
    return pl.pallas_call(
        functools.partial(_decode_kernel, n_heads=n_heads, pages_per_step=pp),
        grid_spec=grid_spec,
        out_shape=jax.ShapeDtypeStruct((n_b, n_rows, dh), F32),
        compiler_params=_params("arbitrary", "arbitrary"),
        name="fox_attend_sample",
    )(page_table, q, k_new, v_new, lf_new,
      *([cache_k] * pp), *([cache_v] * pp), *([cache_lf] * pp))


def kernel(x_prompt, x_sample, cache_k, cache_v, cache_logf, state_conv, page_table, norm_g,
           w_ffn_up, w_ffn_down, w_fox_in, b_fox_f, w_fox_out, w_conv_in, w_conv_w, w_conv_out):
    n_b, seq, d = x_prompt.shape
    n_db, n_tok, _ = x_sample.shape
    depth = norm_g.shape[0]
    n_fox = w_fox_in.shape[0]
    n_heads = b_fox_f.shape[1]
    dh = d // n_heads
    page = cache_k.shape[2]
    attn_scale = dh ** -0.5
    conv_width = w_conv_w.shape[1]

    tm_p = 512
    rows_p = n_b * seq
    rows_s = n_db * n_tok

    hp = x_prompt.reshape(rows_p, d)
    hs = x_sample.reshape(rows_s, d)

    keys_per_row = V7X_LANES // n_heads
    n_pool = cache_k.shape[1]
    cache_k2 = cache_k.reshape(n_fox, n_pool, page * n_heads, dh)
    cache_v2 = cache_v.reshape(n_fox, n_pool, page * n_heads, dh)
    cache_lf = cache_logf.reshape(n_fox, n_pool, page // keys_per_row, V7X_LANES)

    ffn_w = (w_ffn_up[0, 0].astype(BF16), w_ffn_down[0, 0].astype(BF16))

    def ffn_pair(hp, hs, g_pre, g_post, ffn_w, nxt):
        cast = None if nxt is None else (w_ffn_up, w_ffn_down) + nxt
        outs = _ffn_half(hp, g_pre, g_post, *ffn_w, tm=tm_p, tf=512, cast_next=cast)
        hs, = _ffn_half(hs, g_pre, g_post, *ffn_w, tm=rows_s, tf=1024)
        return outs[0], hs, tuple(outs[1:])

    w_fox_in_bf = w_fox_in.astype(BF16)
    w_fox_out_bf = w_fox_out.astype(BF16)
    w_conv_in_bf = w_conv_in.astype(BF16)
    w_conv_out_bf = w_conv_out.astype(BF16)
    w_fgate = jnp.pad(w_fox_in[:, :, 3 * d:], ((0, 0), (0, 0), (0, V7X_LANES - n_heads)))
    w_fgate = w_fgate.astype(BF16)
    b_fgate = jnp.pad(b_fox_f, ((0, 0), (0, V7X_LANES - n_heads)))[:, None, :]

    k_all = jnp.zeros((n_fox, rows_p, n_heads, dh), F32)
    v_all = jnp.zeros((n_fox, rows_p, n_heads, dh), F32)
    lf_all = jnp.zeros((n_fox, rows_p, V7X_LANES), F32)
    ksl, vsl, fsl, cp, csl = [], [], [], [], []
    for i in range(depth):
        g = norm_g[i][:, None, :]
        mi = i // 2

        hp, hs, ffn_w = ffn_pair(hp, hs, g[0], g[1], ffn_w, (i, 1))

        if i % 2 == 0:
            bf = b_fgate[mi]

            q = _norm_matmul(hp, g[2], w_fox_in_bf, layer=mi, tm=tm_p, n=d, col_block=0,
                             out_dtype=BF16, scale=attn_scale * LOG2_E)
            k_all, k_bf = _kv_project(hp, g[2], w_fox_in_bf, k_all, layer=mi, tm=tm_p,
                                      col_block=1)
            v_all, v_bf = _kv_project(hp, g[2], w_fox_in_bf, v_all, layer=mi, tm=tm_p,
                                      col_block=2)
            lf_all = _norm_matmul(hp, g[2], w_fgate, layer=mi, tm=tm_p, n=V7X_LANES, bias=bf,
                                  stack=(mi, lf_all))
            cs = _cumsum_heads(lf_all, layer=mi, batch=n_b, seq=seq, n_heads=n_heads, tc=512)
            o = _attend_prompt(q, k_bf, v_bf, cs, batch=n_b, seq=seq, n_heads=n_heads,
                               tile=512)
            hp = _matmul_norm_residual(o, w_fox_out_bf, g[3], hp, layer=mi, tm=tm_p)

            q = _norm_matmul(hs, g[2], w_fox_in_bf, layer=mi, tm=rows_s, n=d, col_block=0,
                             scale=attn_scale)
            k = _norm_matmul(hs, g[2], w_fox_in_bf, layer=mi, tm=rows_s, n=d, col_block=1)
            v = _norm_matmul(hs, g[2], w_fox_in_bf, layer=mi, tm=rows_s, n=d, col_block=2)
            lf = _norm_matmul(hs, g[2], w_fgate, layer=mi, tm=rows_s, n=V7X_LANES, bias=bf)
            pad = ((0, 0), (0, keys_per_row - n_tok), (0, 0), (0, 0))
            k_new = jnp.pad(k.reshape(n_db, n_tok, n_heads, dh), pad)
            v_new = jnp.pad(v.reshape(n_db, n_tok, n_heads, dh), pad)
            lf_new = lf[:, :n_heads].reshape(n_db, n_tok, n_heads)
            o = _attend_sample(q.reshape(n_db, n_tok * n_heads, dh),
                               k_new.reshape(n_db, V7X_LANES, dh),
                               v_new.reshape(n_db, V7X_LANES, dh),
                               jnp.pad(lf_new, pad[:3]).reshape(n_db, 1, V7X_LANES),
                               cache_k2, cache_v2, cache_lf, page_table,
                               layer=mi, n_heads=n_heads, pages_per_step=8)
            hs = _matmul_norm_residual(o.reshape(rows_s, d), w_fox_out_bf, g[3], hs,
                                       layer=mi, tm=rows_s)
            ksl.append(k.reshape(n_db, n_tok, n_heads, dh))
            vsl.append(v.reshape(n_db, n_tok, n_heads, dh))
            fsl.append(lf_new)
        else:
            cw = w_conv_w[mi]

            state8 = jnp.zeros((n_b, V7X_SUBLANES, d), F32)
            y, tail = _conv_prompt(hp, g[2], w_conv_in_bf, cw, state8, layer=mi, batch=n_b,
                                   seq=seq, tm=tm_p, tn=512)
            hp = _matmul_norm_residual(y, w_conv_out_bf, g[3], hp, layer=mi, tm=tm_p)
            tail = jnp.swapaxes(tail[:, :, V7X_SUBLANES - (conv_width - 1):], 1, 2)
            cp.append(tail.reshape(n_b, conv_width - 1, d))

            st = state_conv[mi]
            zero = jnp.zeros((n_db, n_tok, d), F32)
            init1 = zero.at[:, 0].set(st[:, 1]).reshape(rows_s, d)
            init2 = zero.at[:, 0].set(st[:, 0]).at[:, 1].set(st[:, 1]).reshape(rows_s, d)
            y, u = _conv_sample(hs, g[2], w_conv_in_bf, cw, init1, init2, layer=mi,
                                seq=n_tok, tn=512)
            hs = _matmul_norm_residual(y, w_conv_out_bf, g[3], hs, layer=mi, tm=rows_s)
            csl.append(u.reshape(n_db, n_tok, d)[:, n_tok - (conv_width - 1):])

        hp, hs, ffn_w = ffn_pair(hp, hs, g[4], g[5], ffn_w,
                                 (i + 1, 0) if i + 1 < depth else None)

    return (hp.reshape(n_b, seq, d), hs.reshape(n_db, n_tok, d),
            k_all.reshape(n_fox, n_b, seq, n_heads, dh),
            v_all.reshape(n_fox, n_b, seq, n_heads, dh),
            lf_all[:, :, :n_heads].reshape(n_fox, n_b, seq, n_heads),
            jnp.stack(cp), jnp.stack(ksl), jnp.stack(vsl), jnp.stack(fsl), jnp.stack(csl))
```

```python
import functools
import math

import jax
import jax.numpy as jnp
from jax import lax
from jax.experimental import pallas as pl
from jax.experimental.pallas import tpu as pltpu

F32 = jnp.float32
BF16 = jnp.bfloat16

NORM_EPS = 1e-6
MACARON_SCALE = 0.5
MASK_VALUE = -1e30
LOG2_E = math.log2(math.e)
V7X_LANES = 128
V7X_SUBLANES = 8
V7X_VMEM_LIMIT_BYTES = 56 * 1024 * 1024
NT_DIMS = (((1,), (1,)), ((), ()))


def _params(*semantics):
    return pltpu.CompilerParams(dimension_semantics=semantics,
                                vmem_limit_bytes=V7X_VMEM_LIMIT_BYTES)


def _rms(x, g):
    return x * lax.rsqrt(jnp.mean(x * x, axis=-1, keepdims=True) + NORM_EPS) * g


def _ffn_kernel(x_ref, gpre_ref, gpost_ref, wg_ref, wu_ref, wd_ref, *rest, cast_next):
    j = pl.program_id(1)
    last = pl.num_programs(1) - 1
    if cast_next:
        next_up_ref, next_down_ref, o_ref, up_bf_ref, down_bf_ref, a_ref, acc_ref = rest
    else:
        o_ref, a_ref, acc_ref = rest

    def partial_out(a):
        if cast_next:
            up_bf_ref[...] = next_up_ref[...].astype(BF16)
            down_bf_ref[...] = next_down_ref[...].astype(BF16)
        gate = jnp.dot(a, wg_ref[...], preferred_element_type=F32)
        up = jnp.dot(a, wu_ref[...], preferred_element_type=F32)
        h = (gate * jax.nn.sigmoid(gate) * up).astype(BF16)
        return jnp.dot(h, wd_ref[...], preferred_element_type=F32)

    @pl.when(j == 0)
    def _():
        a = _rms(x_ref[...], gpre_ref[...]).astype(BF16)
        a_ref[...] = a
        acc_ref[...] = partial_out(a)

    @pl.when(jnp.logical_and(j > 0, j < last))
    def _():
        acc_ref[...] += partial_out(a_ref[...])

    @pl.when(j == last)
    def _():
        y = acc_ref[...] + partial_out(a_ref[...])
        o_ref[...] = x_ref[...] + MACARON_SCALE * _rms(y, gpost_ref[...])


def _ffn_half(x, g_pre, g_post, w_up, w_down, *, tm, tf, cast_next=None):
    rows, d = x.shape
    dff = w_down.shape[0]
    nj = dff // tf
    assert nj >= 2
    in_specs = [
        pl.BlockSpec((tm, d), lambda i, j: (i, 0)),
        pl.BlockSpec((1, d), lambda i, j: (0, 0)),
        pl.BlockSpec((1, d), lambda i, j: (0, 0)),
        pl.BlockSpec((d, tf), lambda i, j: (0, j)),
        pl.BlockSpec((d, tf), lambda i, j: (0, j + nj)),
        pl.BlockSpec((tf, d), lambda i, j: (j, 0)),
    ]
    args = [x, g_pre, g_post, w_up, w_up, w_down]
    out_specs = [pl.BlockSpec((tm, d), lambda i, j: (i, 0))]
    out_shape = [jax.ShapeDtypeStruct((rows, d), F32)]
    if cast_next is not None:
        up_f32, down_f32, layer, half = cast_next
        n_steps = (rows // tm) * nj
        up_rows, down_rows = d // n_steps, dff // n_steps
        in_specs += [
            pl.BlockSpec((None, None, up_rows, 2 * dff), lambda i, j: (layer, half, i * nj + j, 0)),
            pl.BlockSpec((None, None, down_rows, d), lambda i, j: (layer, half, i * nj + j, 0)),
        ]
        args += [up_f32, down_f32]
        out_specs += [pl.BlockSpec((up_rows, 2 * dff), lambda i, j: (i * nj + j, 0)),
                      pl.BlockSpec((down_rows, d), lambda i, j: (i * nj + j, 0))]
        out_shape += [jax.ShapeDtypeStruct((d, 2 * dff), BF16),
                      jax.ShapeDtypeStruct((dff, d), BF16)]
    return pl.pallas_call(
        functools.partial(_ffn_kernel, cast_next=cast_next is not None),
        grid=(rows // tm, nj),
        in_specs=in_specs,
        out_specs=out_specs,
        out_shape=out_shape,
        scratch_shapes=[pltpu.VMEM((tm, d), BF16), pltpu.VMEM((tm, d), F32)],
        compiler_params=_params("arbitrary", "arbitrary"),
        name="ffn_half",
    )(*args)


def _nm_kernel(x_ref, g_ref, w_ref, b_ref, *rest, scale, log_sigmoid):
    o_ref = rest[-1]
    a = _rms(x_ref[...], g_ref[...]).astype(BF16)
    y = jnp.dot(a, w_ref[...], preferred_element_type=F32)
    if log_sigmoid:
        y = jax.nn.log_sigmoid(y + b_ref[...])
    if scale is not None:
        y = y * scale
    o_ref[...] = y.astype(o_ref.dtype)


def _norm_matmul(x, g, w, *, layer, tm, n, col_block=0, out_dtype=F32, scale=None, bias=None,
                 stack=None):
    rows, d = x.shape
    log_sigmoid = bias is not None
    if bias is None:
        bias = jnp.zeros((1, n), F32)
    in_specs = [
        pl.BlockSpec((tm, d), lambda i: (i, 0)),
        pl.BlockSpec((1, d), lambda i: (0, 0)),
        pl.BlockSpec((None, d, n), lambda i: (layer, 0, col_block)),
        pl.BlockSpec((1, n), lambda i: (0, 0)),
    ]
    args = [x, g, w, bias]
    aliases = {}
    if stack is None:
        out_spec = pl.BlockSpec((tm, n), lambda i: (i, 0))
        out_shape = jax.ShapeDtypeStruct((rows, n), out_dtype)
    else:
        slab, stacked = stack
        out_spec = pl.BlockSpec((None, tm, n), lambda i: (slab, i, 0))
        out_shape = jax.ShapeDtypeStruct(stacked.shape, stacked.dtype)
        in_specs.append(pl.BlockSpec(memory_space=pl.ANY))
        args.append(stacked)
        aliases = {len(args) - 1: 0}
    return pl.pallas_call(
        functools.partial(_nm_kernel, scale=scale, log_sigmoid=log_sigmoid),
        grid=(rows // tm,),
        in_specs=in_specs,
        out_specs=out_spec,
        out_shape=out_shape,
        input_output_aliases=aliases,
        compiler_params=_params("arbitrary"),
        name="norm_matmul",
    )(*args)


def _kv_kernel(x_ref, g_ref, w_ref, stacked_ref, o_heads_ref, o_flat_ref):
    del stacked_ref
    n_heads, dh = o_heads_ref.shape[1:]
    a = _rms(x_ref[...], g_ref[...]).astype(BF16)
    y = jnp.dot(a, w_ref[...], preferred_element_type=F32)
    o_flat_ref[...] = y.astype(o_flat_ref.dtype)
    for h in range(n_heads):
        o_heads_ref[:, h, :] = y[:, h * dh:(h + 1) * dh]


def _kv_project(x, g, w, stacked, *, layer, tm, col_block):
    rows, d = x.shape
    n_heads, dh = stacked.shape[2:]
    return pl.pallas_call(
        _kv_kernel,
        grid=(rows // tm,),
        in_specs=[
            pl.BlockSpec((tm, d), lambda i: (i, 0)),
            pl.BlockSpec((1, d), lambda i: (0, 0)),
            pl.BlockSpec((None, d, d), lambda i: (layer, 0, col_block)),
            pl.BlockSpec(memory_space=pl.ANY),
        ],
        out_specs=[pl.BlockSpec((None, tm, n_heads, dh), lambda i: (layer, i, 0, 0)),
                   pl.BlockSpec((tm, d), lambda i: (i, 0))],
        out_shape=[jax.ShapeDtypeStruct(stacked.shape, stacked.dtype),
                   jax.ShapeDtypeStruct((rows, d), BF16)],
        input_output_aliases={3: 0},
        compiler_params=_params("arbitrary"),
        name="kv_project",
    )(x, g, w, stacked)


def _mnr_kernel(y_ref, w_ref, g_ref, res_ref, o_ref):
    z = jnp.dot(y_ref[...].astype(BF16), w_ref[...], preferred_element_type=F32)
    o_ref[...] = res_ref[...] + _rms(z, g_ref[...])


def _matmul_norm_residual(y, w, g, res, *, layer, tm):
    rows, d = res.shape
    k = w.shape[1]
    return pl.pallas_call(
        _mnr_kernel,
        grid=(rows // tm,),
        in_specs=[
            pl.BlockSpec((tm, k), lambda i: (i, 0)),
            pl.BlockSpec((None, k, d), lambda i: (layer, 0, 0)),
            pl.BlockSpec((1, d), lambda i: (0, 0)),
            pl.BlockSpec((tm, d), lambda i: (i, 0)),
        ],
        out_specs=pl.BlockSpec((tm, d), lambda i: (i, 0)),
        out_shape=jax.ShapeDtypeStruct((rows, d), F32),
        compiler_params=_params("arbitrary"),
        name="matmul_norm_residual",
    )(y, w, g, res)


def _cumsum_kernel(lf_ref, o_ref, carry_ref, *, n_heads):
    tc = lf_ref.shape[0]

    @pl.when(pl.program_id(1) == 0)
    def _():
        carry_ref[...] = jnp.zeros_like(carry_ref)

    lft = lf_ref[...].T
    upper = (lax.broadcasted_iota(jnp.int32, (tc, tc), 0)
             <= lax.broadcasted_iota(jnp.int32, (tc, tc), 1)).astype(F32)
    c = jnp.dot(lft, upper, precision=lax.Precision.HIGHEST,
                preferred_element_type=F32) + carry_ref[...]
    carry_ref[...] = c[:, tc - 1:tc]
    o_ref[0] = c[:n_heads] * LOG2_E


def _cumsum_heads(lf, *, layer, batch, seq, n_heads, tc):
    lanes = lf.shape[2]
    nt = seq // tc
    return pl.pallas_call(
        functools.partial(_cumsum_kernel, n_heads=n_heads),
        grid=(batch, nt),
        in_specs=[pl.BlockSpec((None, tc, lanes), lambda b, t: (layer, b * nt + t, 0))],
        out_specs=pl.BlockSpec((1, n_heads, tc), lambda b, t: (b, 0, t)),
        out_shape=jax.ShapeDtypeStruct((batch, n_heads, seq), F32),
        scratch_shapes=[pltpu.VMEM((lanes, 1), F32)],
        compiler_params=_params("arbitrary", "arbitrary"),
        name="logf_cumsum",
    )(lf)


def _attn_kernel(q_ref, k_ref, v_ref, cs_ref, o_ref, *, tile):
    seq, dh = q_ref.shape
    h = pl.program_id(1)
    cs = cs_ref[0, pl.ds(h, 1), :]
    row = lax.broadcasted_iota(jnp.int32, (tile, tile), 0)
    col = lax.broadcasted_iota(jnp.int32, (tile, tile), 1)
    causal = col <= row
    for qi in range(seq // tile):
        q = q_ref[qi * tile:(qi + 1) * tile, :]
        m = jnp.full((tile, 1), MASK_VALUE, F32)
        l = jnp.zeros((tile, 1), F32)
        acc = jnp.zeros((tile, dh), F32)
        for j in range(qi + 1):
            keys = slice(j * tile, (j + 1) * tile)
            s = lax.dot_general(q, k_ref[keys, :], NT_DIMS, preferred_element_type=F32)
            s = s - cs[:, keys]
            if j == qi:
                s = jnp.where(causal, s, MASK_VALUE)
            m_new = jnp.maximum(m, jnp.max(s, axis=-1, keepdims=True))
            alpha = jnp.exp2(m - m_new)
            p = jnp.exp2(s - m_new)
            l = alpha * l + jnp.sum(p, axis=-1, keepdims=True)
            acc = alpha * acc + jnp.dot(p.astype(BF16), v_ref[keys, :],
                                        preferred_element_type=F32)
            m = m_new
        o_ref[qi * tile:(qi + 1) * tile, :] = (acc / l).astype(o_ref.dtype)


def _attend_prompt(q, k, v, cs, *, batch, seq, n_heads, tile):
    rows, d = q.shape
    dh = d // n_heads
    return pl.pallas_call(
        functools.partial(_attn_kernel, tile=tile),
        grid=(batch, n_heads),
        in_specs=[
            pl.BlockSpec((seq, dh), lambda b, h: (b, h)),
            pl.BlockSpec((seq, dh), lambda b, h: (b, h)),
            pl.BlockSpec((seq, dh), lambda b, h: (b, h)),
            pl.BlockSpec((1, n_heads, seq), lambda b, h: (b, 0, 0)),
        ],
        out_specs=pl.BlockSpec((seq, dh), lambda b, h: (b, h)),
        out_shape=jax.ShapeDtypeStruct((rows, d), BF16),
        compiler_params=_params("arbitrary", "arbitrary"),
        name="fox_attend_prompt",
    )(q, k, v, cs)


def _conv_mix(u, um1, um2, b, cw):
    return b * (cw[0:1] * um2 + cw[1:2] * um1 + cw[2:3] * u)


def _conv_prompt_kernel(x_ref, g_ref, wb_ref, wc_ref, wh_ref, cw_ref, st_ref,
                        y_ref, tail_ref, a_ref, carry_ref, *, tiles_per_seq):
    tm, tn = y_ref.shape
    i = pl.program_id(0)
    j = pl.program_id(1)

    @pl.when(j == 0)
    def _():
        a_ref[...] = _rms(x_ref[...], g_ref[...]).astype(BF16)

    a = a_ref[...]
    b = jnp.dot(a, wb_ref[...], preferred_element_type=F32)
    c = jnp.dot(a, wc_ref[...], preferred_element_type=F32)
    hh = jnp.dot(a, wh_ref[...], preferred_element_type=F32)
    u = c * hh

    @pl.when(i % tiles_per_seq == 0)
    def _():
        carry_ref[j] = st_ref[0]

    prev2 = carry_ref[j, V7X_SUBLANES - 2:V7X_SUBLANES - 1, :]
    prev1 = carry_ref[j, V7X_SUBLANES - 1:V7X_SUBLANES, :]
    row = lax.broadcasted_iota(jnp.int32, (tm, tn), 0)
    um1 = jnp.where(row == 0, prev1, pltpu.roll(u, 1, 0))
    um2 = jnp.where(row == 0, prev2, jnp.where(row == 1, prev1, pltpu.roll(u, 2, 0)))
    y_ref[...] = _conv_mix(u, um1, um2, b, cw_ref[...]).astype(y_ref.dtype)
    tail = u[tm - V7X_SUBLANES:tm, :]
    carry_ref[j] = tail
    tail_ref[0, j] = tail


def _conv_prompt(x, g, w_in, conv_w, state8, *, layer, batch, seq, tm, tn):
    rows, d = x.shape
    nj = d // tn
    tiles_per_seq = seq // tm
    return pl.pallas_call(
        functools.partial(_conv_prompt_kernel, tiles_per_seq=tiles_per_seq),
        grid=(rows // tm, nj),
        in_specs=[
            pl.BlockSpec((tm, d), lambda i, j: (i, 0)),
            pl.BlockSpec((1, d), lambda i, j: (0, 0)),
            pl.BlockSpec((None, d, tn), lambda i, j: (layer, 0, j)),
            pl.BlockSpec((None, d, tn), lambda i, j: (layer, 0, j + nj)),
            pl.BlockSpec((None, d, tn), lambda i, j: (layer, 0, j + 2 * nj)),
            pl.BlockSpec((conv_w.shape[0], tn), lambda i, j: (0, j)),
            pl.BlockSpec((1, V7X_SUBLANES, tn), lambda i, j: (i // tiles_per_seq, 0, j)),
        ],
        out_specs=[
            pl.BlockSpec((tm, tn), lambda i, j: (i, j)),
            pl.BlockSpec((1, nj, V7X_SUBLANES, tn), lambda i, j: (i // tiles_per_seq, 0, 0, 0)),
        ],
        out_shape=[jax.ShapeDtypeStruct((rows, d), BF16),
                   jax.ShapeDtypeStruct((batch, nj, V7X_SUBLANES, tn), F32)],
        scratch_shapes=[pltpu.VMEM((tm, d), BF16),
                        pltpu.VMEM((nj, V7X_SUBLANES, tn), F32)],
        compiler_params=_params("arbitrary", "arbitrary"),
        name="conv_mix_prompt",
    )(x, g, w_in, w_in, w_in, conv_w, state8)


def _conv_sample_kernel(x_ref, g_ref, wb_ref, wc_ref, wh_ref, cw_ref, init1_ref, init2_ref,
                        y_ref, u_ref, *, seq):
    rows, tn = y_ref.shape
    a = _rms(x_ref[...], g_ref[...]).astype(BF16)
    b = jnp.dot(a, wb_ref[...], preferred_element_type=F32)
    c = jnp.dot(a, wc_ref[...], preferred_element_type=F32)
    hh = jnp.dot(a, wh_ref[...], preferred_element_type=F32)
    u = c * hh
    pos = lax.rem(lax.broadcasted_iota(jnp.int32, (rows, tn), 0), seq)
    um1 = jnp.where(pos >= 1, pltpu.roll(u, 1, 0), init1_ref[...])
    um2 = jnp.where(pos >= 2, pltpu.roll(u, 2, 0), init2_ref[...])
    y_ref[...] = _conv_mix(u, um1, um2, b, cw_ref[...]).astype(y_ref.dtype)
    u_ref[...] = u


def _conv_sample(x, g, w_in, conv_w, init1, init2, *, layer, seq, tn):
    rows, d = x.shape
    nj = d // tn
    return pl.pallas_call(
        functools.partial(_conv_sample_kernel, seq=seq),
        grid=(nj,),
        in_specs=[
            pl.BlockSpec((rows, d), lambda j: (0, 0)),
            pl.BlockSpec((1, d), lambda j: (0, 0)),
            pl.BlockSpec((None, d, tn), lambda j: (layer, 0, j)),
            pl.BlockSpec((None, d, tn), lambda j: (layer, 0, j + nj)),
            pl.BlockSpec((None, d, tn), lambda j: (layer, 0, j + 2 * nj)),
            pl.BlockSpec((conv_w.shape[0], tn), lambda j: (0, j)),
            pl.BlockSpec((rows, tn), lambda j: (0, j)),
            pl.BlockSpec((rows, tn), lambda j: (0, j)),
        ],
        out_specs=[pl.BlockSpec((rows, tn), lambda j: (0, j)),
                   pl.BlockSpec((rows, tn), lambda j: (0, j))],
        out_shape=[jax.ShapeDtypeStruct((rows, d), BF16),
                   jax.ShapeDtypeStruct((rows, d), F32)],
        compiler_params=_params("arbitrary"),
        name="conv_mix_sample",
    )(x, g, w_in, w_in, w_in, conv_w, init1, init2)


def _packed_cumsum(lf, carry, n_heads):
    rows, lanes = lf.shape
    lane = lax.broadcasted_iota(jnp.int32, lf.shape, 1)
    x = lf
    shift = n_heads
    while shift < lanes:
        x = x + jnp.where(lane >= shift, pltpu.roll(x, shift, 1), 0.0)
        shift *= 2
    tot = jnp.where(lane >= lanes - n_heads, x, 0.0)
    shift = n_heads
    while shift < lanes:
        tot = tot + pltpu.roll(tot, shift, 1)
        shift *= 2
    incl = tot
    if rows > 1:
        row = lax.broadcasted_iota(jnp.int32, lf.shape, 0)
        shift = 1
        while shift < rows:
            incl = incl + jnp.where(row >= shift, pltpu.roll(incl, shift, 0), 0.0)
            shift *= 2
    cum = x + (incl - tot) + carry
    return cum, incl[rows - 1:rows, :] + carry


def _decode_kernel(pt_ref, q_ref, kn_ref, vn_ref, lfn_ref, *rest, n_heads, pages_per_step):
    del pt_ref
    pp = pages_per_step
    k_refs, v_refs, lf_refs = rest[:pp], rest[pp:2 * pp], rest[2 * pp:3 * pp]
    o_ref, m_ref, l_ref, acc_ref, carry_ref = rest[3 * pp:]
    n_rows, dh = q_ref.shape[1], q_ref.shape[2]
    lanes = lf_refs[0].shape[3]
    p_idx = pl.program_id(1)

    @pl.when(p_idx == 0)
    def _():
        m_ref[...] = jnp.full_like(m_ref, MASK_VALUE)
        l_ref[...] = jnp.zeros_like(l_ref)
        acc_ref[...] = jnp.zeros_like(acc_ref)
        carry_ref[...] = jnp.zeros_like(carry_ref)

    row = lax.broadcasted_iota(jnp.int32, (n_rows, lanes), 0)
    lane = lax.broadcasted_iota(jnp.int32, (n_rows, lanes), 1)
    same_head = lax.rem(row, n_heads) == lax.rem(lane, n_heads)

    def process(k_blocks, v_blocks, lf, valid):
        q = q_ref[0]
        cum, carry = _packed_cumsum(lf, carry_ref[...], n_heads)
        carry_ref[...] = carry
        parts = []
        c_row = 0
        for kb in k_blocks:
            s = lax.dot_general(q, kb, NT_DIMS, preferred_element_type=F32)
            for j in range(kb.shape[0] // lanes):
                sj = s[:, j * lanes:(j + 1) * lanes] - cum[c_row:c_row + 1, :]
                parts.append(jnp.where(valid, sj, MASK_VALUE))
                c_row += 1
        s = jnp.concatenate(parts, axis=1)
        m_old = m_ref[...]
        m_new = jnp.maximum(m_old, jnp.max(s, axis=-1, keepdims=True))
        alpha = jnp.exp(m_old - m_new)
        p = jnp.exp(s - m_new)
        l_ref[...] = alpha * l_ref[...] + jnp.sum(p, axis=-1, keepdims=True)
        pv = None
        col = 0
        for vb in v_blocks:
            part = jnp.dot(p[:, col:col + vb.shape[0]], vb, preferred_element_type=F32)
            pv = part if pv is None else pv + part
            col += vb.shape[0]
        acc_ref[...] = alpha * acc_ref[...] + pv
        m_ref[...] = m_new

    process([r[0, 0] for r in k_refs], [r[0, 0] for r in v_refs],
            jnp.concatenate([r[0, 0] for r in lf_refs], axis=0), same_head)

    @pl.when(p_idx == pl.num_programs(1) - 1)
    def _():
        causal = (lane // n_heads) <= (row // n_heads)
        process([kn_ref[0]], [vn_ref[0]], lfn_ref[0], same_head & causal)
        o_ref[0] = acc_ref[...] / l_ref[...]


def _attend_sample(q, k_new, v_new, lf_new, cache_k, cache_v, cache_lf, page_table, *,
                   layer, n_heads, pages_per_step):
    n_b, n_rows, dh = q.shape
    page_rows = cache_k.shape[2]
    lf_rows, lanes = cache_lf.shape[2:]
    n_pages = page_table.shape[1]
    pp = pages_per_step
    n_steps = n_pages // pp

    def page_spec(rows, cols, r):
        return pl.BlockSpec((1, 1, rows, cols),
                            lambda b, p, pt: (layer, pt[b, p * pp + r], 0, 0))

    in_specs = [
        pl.BlockSpec((1, n_rows, dh), lambda b, p, pt: (b, 0, 0)),
        pl.BlockSpec((1, lanes, dh), lambda b, p, pt: (b, 0, 0)),
        pl.BlockSpec((1, lanes, dh), lambda b, p, pt: (b, 0, 0)),
        pl.BlockSpec((1, 1, lanes), lambda b, p, pt: (b, 0, 0)),
    ]
    in_specs += [page_spec(page_rows, dh, r) for r in range(pp)]
    in_specs += [page_spec(page_rows, dh, r) for r in range(pp)]
    in_specs += [page_spec(lf_rows, lanes, r) for r in range(pp)]
    grid_spec = pltpu.PrefetchScalarGridSpec(
        num_scalar_prefetch=1,
        grid=(n_b, n_steps),
        in_specs=in_specs,
        out_specs=pl.BlockSpec((1, n_rows, dh), lambda b, p, pt: (b, 0, 0)),
        scratch_shapes=[
            pltpu.VMEM((n_rows, 1), F32),
            pltpu.VMEM((n_rows, 1), F32),
            pltpu.VMEM((n_rows, dh), F32),
            pltpu.VMEM((1, lanes), F32),
        ],
    )
    return pl.pallas_call(
        functools.partial(_decode_kernel, n_heads=n_heads, pages_per_step=pp),
        grid_spec=grid_spec,
        out_shape=jax.ShapeDtypeStruct((n_b, n_rows, dh), F32),
        compiler_params=_params("arbitrary", "arbitrary"),
        name="fox_attend_sample",
    )(page_table, q, k_new, v_new, lf_new,
      *([cache_k] * pp), *([cache_v] * pp), *([cache_lf] * pp))


def kernel(x_prompt, x_sample, cache_k, cache_v, cache_logf, state_conv, page_table, norm_g,
           w_ffn_up, w_ffn_down, w_fox_in, b_fox_f, w_fox_out, w_conv_in, w_conv_w, w_conv_out):
    n_b, seq, d = x_prompt.shape
    n_db, n_tok, _ = x_sample.shape
    depth = norm_g.shape[0]
    n_fox = w_fox_in.shape[0]
    n_heads = b_fox_f.shape[1]
    dh = d // n_heads
    page = cache_k.shape[2]
    attn_scale = dh ** -0.5
    conv_width = w_conv_w.shape[1]

    tm_p = 512
    rows_p = n_b * seq
    rows_s = n_db * n_tok

    hp = x_prompt.reshape(rows_p, d)
    hs = x_sample.reshape(rows_s, d)

    keys_per_row = V7X_LANES // n_heads
    n_pool = cache_k.shape[1]
    cache_k2 = cache_k.reshape(n_fox, n_pool, page * n_heads, dh)
    cache_v2 = cache_v.reshape(n_fox, n_pool, page * n_heads, dh)
    cache_lf = cache_logf.reshape(n_fox, n_pool, page // keys_per_row, V7X_LANES)

    ffn_w = (w_ffn_up[0, 0].astype(BF16), w_ffn_down[0, 0].astype(BF16))

    def ffn_pair(hp, hs, g_pre, g_post, ffn_w, nxt):
        cast = None if nxt is None else (w_ffn_up, w_ffn_down) + nxt
        outs = _ffn_half(hp, g_pre, g_post, *ffn_w, tm=tm_p, tf=1024, cast_next=cast)
        hs, = _ffn_half(hs, g_pre, g_post, *ffn_w, tm=rows_s, tf=1024)
        return outs[0], hs, tuple(outs[1:])

    w_fox_in_bf = w_fox_in.astype(BF16)
    w_fox_out_bf = w_fox_out.astype(BF16)
    w_conv_in_bf = w_conv_in.astype(BF16)
    w_conv_out_bf = w_conv_out.astype(BF16)
    w_fgate = jnp.pad(w_fox_in[:, :, 3 * d:], ((0, 0), (0, 0), (0, V7X_LANES - n_heads)))
    w_fgate = w_fgate.astype(BF16)
    b_fgate = jnp.pad(b_fox_f, ((0, 0), (0, V7X_LANES - n_heads)))[:, None, :]

    k_all = jnp.zeros((n_fox, rows_p, n_heads, dh), F32)
    v_all = jnp.zeros((n_fox, rows_p, n_heads, dh), F32)
    lf_all = jnp.zeros((n_fox, rows_p, V7X_LANES), F32)
    ksl, vsl, fsl, cp, csl = [], [], [], [], []
    for i in range(depth):
        g = norm_g[i][:, None, :]
        mi = i // 2

        hp, hs, ffn_w = ffn_pair(hp, hs, g[0], g[1], ffn_w, (i, 1))

        if i % 2 == 0:
            bf = b_fgate[mi]

            q = _norm_matmul(hp, g[2], w_fox_in_bf, layer=mi, tm=tm_p, n=d, col_block=0,
                             out_dtype=BF16, scale=attn_scale * LOG2_E)
            k_all, k_bf = _kv_project(hp, g[2], w_fox_in_bf, k_all, layer=mi, tm=tm_p,
                                      col_block=1)
            v_all, v_bf = _kv_project(hp, g[2], w_fox_in_bf, v_all, layer=mi, tm=tm_p,
                                      col_block=2)
            lf_all = _norm_matmul(hp, g[2], w_fgate, layer=mi, tm=tm_p, n=V7X_LANES, bias=bf,
                                  stack=(mi, lf_all))
            cs = _cumsum_heads(lf_all, layer=mi, batch=n_b, seq=seq, n_heads=n_heads, tc=512)
            o = _attend_prompt(q, k_bf, v_bf, cs, batch=n_b, seq=seq, n_heads=n_heads,
                               tile=512)
            hp = _matmul_norm_residual(o, w_fox_out_bf, g[3], hp, layer=mi, tm=tm_p)

            q = _norm_matmul(hs, g[2], w_fox_in_bf, layer=mi, tm=rows_s, n=d, col_block=0,
                             scale=attn_scale)
            k = _norm_matmul(hs, g[2], w_fox_in_bf, layer=mi, tm=rows_s, n=d, col_block=1)
            v = _norm_matmul(hs, g[2], w_fox_in_bf, layer=mi, tm=rows_s, n=d, col_block=2)
            lf = _norm_matmul(hs, g[2], w_fgate, layer=mi, tm=rows_s, n=V7X_LANES, bias=bf)
            pad = ((0, 0), (0, keys_per_row - n_tok), (0, 0), (0, 0))
            k_new = jnp.pad(k.reshape(n_db, n_tok, n_heads, dh), pad)
            v_new = jnp.pad(v.reshape(n_db, n_tok, n_heads, dh), pad)
            lf_new = lf[:, :n_heads].reshape(n_db, n_tok, n_heads)
            o = _attend_sample(q.reshape(n_db, n_tok * n_heads, dh),
                               k_new.reshape(n_db, V7X_LANES, dh),
                               v_new.reshape(n_db, V7X_LANES, dh),
                               jnp.pad(lf_new, pad[:3]).reshape(n_db, 1, V7X_LANES),
                               cache_k2, cache_v2, cache_lf, page_table,
                               layer=mi, n_heads=n_heads, pages_per_step=8)
            hs = _matmul_norm_residual(o.reshape(rows_s, d), w_fox_out_bf, g[3], hs,
                                       layer=mi, tm=rows_s)
            ksl.append(k.reshape(n_db, n_tok, n_heads, dh))
            vsl.append(v.reshape(n_db, n_tok, n_heads, dh))
            fsl.append(lf_new)
        else:
            cw = w_conv_w[mi]

            state8 = jnp.zeros((n_b, V7X_SUBLANES, d), F32)
            y, tail = _conv_prompt(hp, g[2], w_conv_in_bf, cw, state8, layer=mi, batch=n_b,
                                   seq=seq, tm=tm_p, tn=512)
            hp = _matmul_norm_residual(y, w_conv_out_bf, g[3], hp, layer=mi, tm=tm_p)
            tail = jnp.swapaxes(tail[:, :, V7X_SUBLANES - (conv_width - 1):], 1, 2)
            cp.append(tail.reshape(n_b, conv_width - 1, d))

            st = state_conv[mi]
            zero = jnp.zeros((n_db, n_tok, d), F32)
            init1 = zero.at[:, 0].set(st[:, 1]).reshape(rows_s, d)
            init2 = zero.at[:, 0].set(st[:, 0]).at[:, 1].set(st[:, 1]).reshape(rows_s, d)
            y, u = _conv_sample(hs, g[2], w_conv_in_bf, cw, init1, init2, layer=mi,
                                seq=n_tok, tn=512)
            hs = _matmul_norm_residual(y, w_conv_out_bf, g[3], hs, layer=mi, tm=rows_s)
            csl.append(u.reshape(n_db, n_tok, d)[:, n_tok - (conv_width - 1):])

        hp, hs, ffn_w = ffn_pair(hp, hs, g[4], g[5], ffn_w,
                                 (i + 1, 0) if i + 1 < depth else None)

    return (hp.reshape(n_b, seq, d), hs.reshape(n_db, n_tok, d),
            k_all.reshape(n_fox, n_b, seq, n_heads, dh),
            v_all.reshape(n_fox, n_b, seq, n_heads, dh),
            lf_all[:, :, :n_heads].reshape(n_fox, n_b, seq, n_heads),
            jnp.stack(cp), jnp.stack(ksl), jnp.stack(vsl), jnp.stack(fsl), jnp.stack(csl))
```
